```python
import math
import jax, jax.numpy as jnp
from jax import lax
import numpy as np

D_MODEL = 4096
BATCH = 4
SEQ = 2048
DEPTH = 2

CHUNK = 64
HEAD_DIM = 128
H_A = 16
H_B = 16
W_A = H_A * HEAD_DIM
W_B = H_B * HEAD_DIM
N_IN = 3 * W_A + H_A + 3 * W_B
Q_BLOCK = 128
LEFT_CHUNKS = 8
MAX_REL = 128
N_GROUPS = 4
E_PER_GROUP = 8
TOP_K = 2
D_FF = 512
EPS = 1e-6
NEG_INF = -1e30
ATTN_SCALE = HEAD_DIM ** -0.5

kernel_name = "chunk_causal_fox_relpos_hier_moe_adaln"


def rmsnorm(x, g):
    x32 = x.astype(jnp.float32)
    y = x32 * lax.rsqrt(jnp.mean(x32 * x32, axis=-1, keepdims=True) + EPS)
    return (y * g.astype(jnp.float32)).astype(x.dtype)


def modulate(h, shift, scale):
    return h * (1 + scale[:, None, :]) + shift[:, None, :]


def to_heads(t, n_heads):
    b, s, _ = t.shape
    return t.reshape(b, s, n_heads, HEAD_DIM).transpose(0, 2, 1, 3)


def from_heads(t):
    b, h, s, d = t.shape
    return t.transpose(0, 2, 1, 3).reshape(b, s, h * d)


def forgetting_attention(q, k, v, log_f):
    s_len = q.shape[2]
    cum = jnp.cumsum(log_f, axis=-1)
    outs = []
    for i in range(s_len // Q_BLOCK):
        q0, q1 = i * Q_BLOCK, (i + 1) * Q_BLOCK
        qb = q[:, :, q0:q1]
        kb = k[:, :, :q1]
        vb = v[:, :, :q1]
        s = jnp.einsum('bhqd,bhkd->bhqk', qb, kb).astype(jnp.float32) * ATTN_SCALE
        s = s + (cum[:, :, q0:q1, None] - cum[:, :, None, :q1])
        causal = jnp.arange(q0, q1)[:, None] >= jnp.arange(q1)[None, :]
        s = jnp.where(causal, s, NEG_INF)
        p = jax.nn.softmax(s, axis=-1).astype(v.dtype)
        outs.append(jnp.einsum('bhqk,bhkd->bhqd', p, vb))
    return jnp.concatenate(outs, axis=2)


def chunked_relpos_attention(q, k, v, rel_table):
    b, h, s_len, d = q.shape
    n_chunks = s_len // CHUNK
    pad = LEFT_CHUNKS * CHUNK
    band = (LEFT_CHUNKS + 1) * CHUNK
    k_pad = jnp.pad(k, ((0, 0), (0, 0), (pad, 0), (0, 0)))
    v_pad = jnp.pad(v, ((0, 0), (0, 0), (pad, 0), (0, 0)))
    qi = jnp.arange(CHUNK)
    kj = jnp.arange(band)
    dist = qi[:, None] - kj[None, :] + pad
    rel_idx = jnp.clip(dist, -MAX_REL, MAX_REL) + MAX_REL
    bias = rel_table.astype(jnp.float32)[:, rel_idx]
    q_chunks = q.reshape(b, h, n_chunks, CHUNK, d).transpose(2, 0, 1, 3, 4)

    def one_chunk(args):
        n, qc = args
        start = n * CHUNK
        kb = lax.dynamic_slice_in_dim(k_pad, start, band, axis=2)
        vb = lax.dynamic_slice_in_dim(v_pad, start, band, axis=2)
        s = jnp.einsum('bhqd,bhkd->bhqk', qc, kb).astype(jnp.float32) * ATTN_SCALE
        s = s + bias[None]
        key_pos = start - pad + kj
        s = jnp.where((key_pos >= 0)[None, None, None, :], s, NEG_INF)
        p = jax.nn.softmax(s, axis=-1).astype(vb.dtype)
        return jnp.einsum('bhqk,bhkd->bhqd', p, vb)

    out = lax.map(one_chunk, (jnp.arange(n_chunks), q_chunks))
    return out.transpose(1, 2, 0, 3, 4).reshape(b, h, s_len, d)


def hybrid_mixer(h, w_in, b_forget, rel_bias, w_branch_a, w_branch_b, w_gate, b_gate, w_out):
    proj = h @ w_in
    splits = [W_A, 2 * W_A, 3 * W_A, 3 * W_A + H_A,
              3 * W_A + H_A + W_B, 3 * W_A + H_A + 2 * W_B]
    qa, ka, va, fa, qb, kb, vb = jnp.split(proj, splits, axis=-1)
    log_f = jax.nn.log_sigmoid(fa.astype(jnp.float32) + b_forget.astype(jnp.float32))
    log_f = log_f.transpose(0, 2, 1)
    ya = forgetting_attention(to_heads(qa, H_A), to_heads(ka, H_A), to_heads(va, H_A), log_f)
    yb = chunked_relpos_attention(to_heads(qb, H_B), to_heads(kb, H_B), to_heads(vb, H_B), rel_bias)
    gates = jax.nn.sigmoid(h @ w_gate + b_gate)
    g_a, g_b = jnp.split(gates, 2, axis=-1)
    merged = g_a * (from_heads(ya) @ w_branch_a) + g_b * (from_heads(yb) @ w_branch_b)
    return merged @ w_out


def hierarchical_moe(h, w_rg, b_rg, w_re, b_re, w_eg, w_eu, w_ed):
    b, s, d = h.shape
    t = h.reshape(b * s, d)
    n_tok = t.shape[0]
    g_logits = (t @ w_rg).astype(jnp.float32) + b_rg.astype(jnp.float32)
    g_prob = jax.nn.softmax(g_logits, axis=-1)
    g_sel = jnp.argmax(g_logits, axis=-1)
    g_w = jnp.take_along_axis(g_prob, g_sel[:, None], axis=-1)
    e_logits = ((t @ w_re).astype(jnp.float32) + b_re.astype(jnp.float32))
    e_logits = e_logits.reshape(n_tok, N_GROUPS, E_PER_GROUP)
    e_in = jnp.take_along_axis(e_logits, g_sel[:, None, None], axis=1)[:, 0]
    top_v, top_i = lax.top_k(e_in, TOP_K)
    top_w = jax.nn.softmax(top_v, axis=-1) * g_w
    w_e = jnp.sum(jax.nn.one_hot(top_i, E_PER_GROUP, dtype=jnp.float32) * top_w[..., None], axis=1)
    combine = jax.nn.one_hot(g_sel, N_GROUPS, dtype=jnp.float32)[:, :, None] * w_e[:, None, :]
    out = jnp.zeros((n_tok, d), t.dtype)
    for g in range(N_GROUPS):
        a = jax.nn.silu(jnp.einsum('td,edf->tef', t, w_eg[g])) * jnp.einsum('td,edf->tef', t, w_eu[g])
        a = a * combine[:, g, :, None].astype(a.dtype)
        out = out + jnp.einsum('tef,efd->td', a, w_ed[g])
    return out.reshape(b, s, d)


def setup_inputs(seed: int = 0) -> dict:
    key = jax.random.key(seed)
    ks = jax.random.split(key, 24)
    f32 = jnp.float32

    def nrm(k, shape, std):
        return jax.random.normal(k, shape, f32) * std

    L, D = DEPTH, D_MODEL
    n_rel = 2 * MAX_REL + 1
    return {
        "x": nrm(ks[0], (BATCH, SEQ, D), 1.0),
        "c": nrm(ks[1], (BATCH, D), 1.0),
        "ada_w": nrm(ks[2], (L, D, 6 * D), 0.5 * D ** -0.5),
        "ada_b": nrm(ks[3], (L, 6 * D), 0.02),
        "norm_mix_g": 1.0 + nrm(ks[4], (L, D), 0.1),
        "norm_ffn_g": 1.0 + nrm(ks[5], (L, D), 0.1),
        "w_in": nrm(ks[6], (L, D, N_IN), D ** -0.5),
        "b_forget": 1.0 + nrm(ks[7], (L, H_A), 0.5),
        "rel_bias": nrm(ks[8], (L, H_B, n_rel), 0.5),
        "w_branch_a": nrm(ks[9], (L, W_A, D), W_A ** -0.5),
        "w_branch_b": nrm(ks[10], (L, W_B, D), W_B ** -0.5),
        "w_gate": nrm(ks[11], (L, D, 2 * D), D ** -0.5),
        "b_gate": nrm(ks[12], (L, 2 * D), 0.02),
        "w_out": nrm(ks[13], (L, D, D), D ** -0.5),
        "w_router_group": nrm(ks[14], (L, D, N_GROUPS), D ** -0.5),
        "b_router_group": nrm(ks[15], (L, N_GROUPS), 0.01),
        "w_router_expert": nrm(ks[16], (L, D, N_GROUPS * E_PER_GROUP), D ** -0.5),
        "b_router_expert": nrm(ks[17], (L, N_GROUPS * E_PER_GROUP), 0.01),
        "w_exp_gate": nrm(ks[18], (L, N_GROUPS, E_PER_GROUP, D, D_FF), D ** -0.5),
        "w_exp_up": nrm(ks[19], (L, N_GROUPS, E_PER_GROUP, D, D_FF), D ** -0.5),
        "w_exp_down": nrm(ks[20], (L, N_GROUPS, E_PER_GROUP, D_FF, D), D_FF ** -0.5),
        "final_norm_g": 1.0 + nrm(ks[21], (D,), 0.1),
    }


def reference(x, c, ada_w, ada_b, norm_mix_g, norm_ffn_g, w_in, b_forget, rel_bias,
              w_branch_a, w_branch_b, w_gate, b_gate, w_out,
              w_router_group, b_router_group, w_router_expert, b_router_expert,
              w_exp_gate, w_exp_up, w_exp_down, final_norm_g):
    c_act = jax.nn.silu(c)
    for l in range(DEPTH):
        mod = c_act @ ada_w[l] + ada_b[l]
        sh_m, sc_m, gt_m, sh_f, sc_f, gt_f = jnp.split(mod, 6, axis=-1)
        h = modulate(rmsnorm(x, norm_mix_g[l]), sh_m, sc_m)
        y = hybrid_mixer(h, w_in[l], b_forget[l], rel_bias[l], w_branch_a[l], w_branch_b[l],
                         w_gate[l], b_gate[l], w_out[l])
        x = x + gt_m[:, None, :] * y
        h = modulate(rmsnorm(x, norm_ffn_g[l]), sh_f, sc_f)
        y = hierarchical_moe(h, w_router_group[l], b_router_group[l], w_router_expert[l],
                             b_router_expert[l], w_exp_gate[l], w_exp_up[l], w_exp_down[l])
        x = x + gt_f[:, None, :] * y
    return rmsnorm(x, final_norm_g)
```

```python
import functools

import numpy as np
import jax
import jax.numpy as jnp
from jax import lax
from jax.experimental import pallas as pl
from jax.experimental.pallas import tpu as pltpu

CHUNK = 64
LEFT_CHUNKS = 8
EPS = 1e-6
NEG_INF = -1e30
LANES = 128
HEAD_DIM = 128
V7X_VMEM_LIMIT_BYTES = 60000 * 1024

F32 = jnp.float32
BF16 = jnp.bfloat16


def _params(*sem):
    return pltpu.CompilerParams(dimension_semantics=sem, vmem_limit_bytes=V7X_VMEM_LIMIT_BYTES)


def _tile(n, pref, mult=LANES):
    t = (min(pref, n) // mult) * mult
    while t >= mult:
        if n % t == 0:
            return t
        t -= mult
    return n


def _cast_rows(src_ref, dst_ref, rows_per_step):
    n = src_ref.shape[0]

    def body(r, carry):
        rows = pl.ds(pl.multiple_of(r * rows_per_step, rows_per_step), rows_per_step)
        dst_ref[rows, :] = src_ref[rows, :].astype(dst_ref.dtype)
        return carry

    lax.fori_loop(0, n // rows_per_step, body, 0)


def _mod_kernel(c_ref, w_ref, b_ref, o_ref):
    c = c_ref[...]
    c_act = (c * jax.nn.sigmoid(c)).astype(BF16)
    o_ref[0] = jnp.dot(c_act, w_ref[0].astype(BF16), preferred_element_type=F32) + b_ref[0]


def _adaln_mod(c, ada_w, ada_b):
    n_layers, d, n = ada_w.shape
    b = c.shape[0]
    bp = -(-b // 8) * 8
    c_pad = jnp.pad(c, ((0, bp - b), (0, 0)))
    tn = _tile(n, 512)
    out = pl.pallas_call(
        _mod_kernel,
        grid=(n_layers, n // tn),
        in_specs=[
            pl.BlockSpec((bp, d), lambda l, j: (0, 0)),
            pl.BlockSpec((1, d, tn), lambda l, j: (l, 0, j)),
            pl.BlockSpec((1, 1, tn), lambda l, j: (l, 0, j)),
        ],
        out_specs=pl.BlockSpec((1, bp, tn), lambda l, j: (l, 0, j)),
        out_shape=jax.ShapeDtypeStruct((n_layers, bp, n), F32),
        compiler_params=_params("arbitrary", "arbitrary"),
        name="adaln_mod",
    )(c_pad, ada_w, ada_b.reshape(n_layers, 1, n))
    return out[:, :b]


def _norm_mod_kernel(x_ref, g_ref, sh_ref, sc_ref, o_ref):
    x = x_ref[0]
    ms = jnp.mean(x * x, axis=-1, keepdims=True)
    y = x * lax.rsqrt(ms + EPS) * g_ref[...]
    o_ref[0] = (y * (1.0 + sc_ref[0]) + sh_ref[0]).astype(o_ref.dtype)


def _norm_mod(x, g, shift, scale, out_dtype):
    b, s, d = x.shape
    ts = _tile(s, 256, 8)
    return pl.pallas_call(
        _norm_mod_kernel,
        grid=(b, s // ts),
        in_specs=[
            pl.BlockSpec((1, ts, d), lambda i, j: (i, j, 0)),
            pl.BlockSpec((1, d), lambda i, j: (0, 0)),
            pl.BlockSpec((1, 1, d), lambda i, j: (i, 0, 0)),
            pl.BlockSpec((1, 1, d), lambda i, j: (i, 0, 0)),
        ],
        out_specs=pl.BlockSpec((1, ts, d), lambda i, j: (i, j, 0)),
        out_shape=jax.ShapeDtypeStruct((b, s, d), out_dtype),
        compiler_params=_params("arbitrary", "arbitrary"),
        name="norm_mod",
    )(x, g.reshape(1, d), shift.reshape(b, 1, d), scale.reshape(b, 1, d))


def _inproj_kernel(h_ref, wa_ref, wb_ref, o_ref, wbf_ref, *, n_plain, shift, rows_per_step):
    j = pl.program_id(0)
    i = pl.program_id(1)
    d, tn = wa_ref.shape

    @pl.when(jnp.logical_and(i == 0, j < n_plain))
    def _():
        _cast_rows(wa_ref, wbf_ref, rows_per_step)

    @pl.when(jnp.logical_and(i == 0, j >= n_plain))
    def _():
        lane = lax.broadcasted_iota(jnp.int32, (rows_per_step, LANES), 1)
        keep = lane < (LANES - shift)

        def body(r, carry):
            rows = pl.ds(pl.multiple_of(r * rows_per_step, rows_per_step), rows_per_step)
            for c in range(tn // LANES):
                cur = wa_ref[rows, c * LANES:(c + 1) * LANES]
                if (c + 1) * LANES < tn:
                    nxt = wa_ref[rows, (c + 1) * LANES:(c + 2) * LANES]
                else:
                    nxt = wb_ref[rows, :]
                cur = pltpu.roll(cur, LANES - shift, 1)
                nxt = pltpu.roll(nxt, LANES - shift, 1)
                wbf_ref[rows, c * LANES:(c + 1) * LANES] = jnp.where(keep, cur, nxt).astype(BF16)
            return carry

        lax.fori_loop(0, d // rows_per_step, body, 0)

    o_ref[...] = jnp.dot(h_ref[...], wbf_ref[...], preferred_element_type=F32).astype(o_ref.dtype)


def _in_projection(h2d, w_in, layer, w_a3, n_forget):
    m, d = h2d.shape
    n_out = w_in.shape[2] - n_forget
    tn = _tile(w_a3, 512)
    assert n_out % tn == 0 and w_a3 % tn == 0 and 0 < n_forget < LANES
    tm = _tile(m, 512, 8)
    kern = functools.partial(_inproj_kernel, n_plain=w_a3 // tn, shift=n_forget,
                             rows_per_step=_tile(d, 256, 8))
    return pl.pallas_call(
        kern,
        grid=(n_out // tn, m // tm),
        in_specs=[
            pl.BlockSpec((tm, d), lambda j, i: (i, 0)),
            pl.BlockSpec((None, d, tn), lambda j, i: (layer, 0, j)),
            pl.BlockSpec((None, d, LANES), lambda j, i: (layer, 0, (j + 1) * (tn // LANES))),
        ],
        out_specs=pl.BlockSpec((tm, tn), lambda j, i: (i, j)),
        out_shape=jax.ShapeDtypeStruct((m, n_out), BF16),
        scratch_shapes=[pltpu.VMEM((d, tn), BF16)],
        compiler_params=_params("arbitrary", "arbitrary"),
        name="in_projection",
    )(h2d, w_in, w_in)


def _split3_dot(x, u):
    x1 = x.astype(BF16)
    r1 = x - x1.astype(F32)
    x2 = r1.astype(BF16)
    x3 = (r1 - x2.astype(F32)).astype(BF16)
    ub = u.astype(BF16)
    return (jnp.dot(x1, ub, preferred_element_type=F32)
            + jnp.dot(x2, ub, preferred_element_type=F32)
            + jnp.dot(x3, ub, preferred_element_type=F32))


def _forget_kernel(h_ref, w_ref, b_ref, o_ref, carry_ref, *, hp):
    s = pl.program_id(1)

    @pl.when(s == 0)
    def _():
        carry_ref[...] = jnp.zeros_like(carry_ref)

    ts = h_ref.shape[1]
    f = jnp.dot(h_ref[0], w_ref[...].astype(BF16), preferred_element_type=F32) + b_ref[...]
    log_f = jnp.minimum(f, 0.0) - jnp.log1p(jnp.exp(-jnp.abs(f)))
    log_ft = log_f.T[:hp]
    row = lax.broadcasted_iota(jnp.int32, (ts, ts), 0)
    col = lax.broadcasted_iota(jnp.int32, (ts, ts), 1)
    upper = jnp.where(row <= col, 1.0, 0.0).astype(F32)
    cs = _split3_dot(log_ft, upper) + carry_ref[:, 0:1]
    o_ref[0] = cs
    carry_ref[...] = jnp.broadcast_to(cs[:, ts - 1:ts], carry_ref.shape)


def _forget_cumsum(h, w_in, layer, b_forget, col0):
    b, s, d = h.shape
    n_heads = b_forget.shape[0]
    assert col0 % LANES == 0 and n_heads <= LANES
    hp = -(-n_heads // 8) * 8
    ts = _tile(s, 512)
    bias = jnp.pad(b_forget.astype(F32), (0, LANES - n_heads)).reshape(1, LANES)
    return pl.pallas_call(
        functools.partial(_forget_kernel, hp=hp),
        grid=(b, s // ts),
        in_specs=[
            pl.BlockSpec((1, ts, d), lambda i, j: (i, j, 0)),
            pl.BlockSpec((None, d, LANES), lambda i, j: (layer, 0, col0 // LANES)),
            pl.BlockSpec((1, LANES), lambda i, j: (0, 0)),
        ],
        out_specs=pl.BlockSpec((1, hp, ts), lambda i, j: (i, 0, j)),
        out_shape=jax.ShapeDtypeStruct((b, hp, s), F32),
        scratch_shapes=[pltpu.VMEM((hp, LANES), F32)],
        compiler_params=_params("arbitrary", "arbitrary"),
        name="forget_cumsum",
    )(h, w_in, bias)


def _fox_kernel(q_ref, k_ref, v_ref, cum_ref, o_ref, *, tq, scale):
    head = pl.program_id(1)
    seq = q_ref.shape[1]
    row = lax.broadcasted_iota(jnp.int32, (tq, tq), 0)
    col = lax.broadcasted_iota(jnp.int32, (tq, tq), 1)
    causal = row >= col

    def scores(q, k0):
        k = k_ref[0, pl.ds(k0, tq), :]
        s = lax.dot_general(q, k, (((1,), (1,)), ((), ())), preferred_element_type=F32) * scale
        return s - cum_ref[0, pl.ds(head, 1), pl.ds(k0, tq)]

    def update(carry, s, k0):
        m, l, acc = carry
        m_new = jnp.maximum(m, jnp.max(s, axis=-1, keepdims=True))
        alpha = jnp.exp(m - m_new)
        p = jnp.exp(s - m_new)
        l = alpha * l + jnp.sum(p, axis=-1, keepdims=True)
        v = v_ref[0, pl.ds(k0, tq), :]
        acc = alpha * acc + jnp.dot(p.astype(BF16), v, preferred_element_type=F32)
        return m_new, l, acc

    def q_body(iq, carry):
        q0 = pl.multiple_of(iq * tq, tq)
        q = q_ref[0, pl.ds(q0, tq), :]

        def kv_body(j, c):
            k0 = pl.multiple_of(j * tq, tq)
            return update(c, scores(q, k0), k0)

        init = (jnp.full((tq, 1), NEG_INF, F32), jnp.zeros((tq, 1), F32),
                jnp.zeros((tq, HEAD_DIM), F32))
        c = lax.fori_loop(0, iq, kv_body, init)
        s_diag = jnp.where(causal, scores(q, q0), NEG_INF)
        _, l, acc = update(c, s_diag, q0)
        o_ref[0, pl.ds(q0, tq), :] = (acc / l).astype(o_ref.dtype)
        return carry

    lax.fori_loop(0, seq // tq, q_body, 0)


def _fox_attention(qkv, cum, n_heads, col_blocks):
    b, s, _ = qkv.shape
    hp = cum.shape[1]
    tq = _tile(s, 256)
    qc, kc, vc = col_blocks
    kern = functools.partial(_fox_kernel, tq=tq, scale=HEAD_DIM ** -0.5)
    return pl.pallas_call(
        kern,
        grid=(b, n_heads),
        in_specs=[
            pl.BlockSpec((1, s, HEAD_DIM), lambda i, h: (i, 0, qc + h)),
            pl.BlockSpec((1, s, HEAD_DIM), lambda i, h: (i, 0, kc + h)),
            pl.BlockSpec((1, s, HEAD_DIM), lambda i, h: (i, 0, vc + h)),
            pl.BlockSpec((1, hp, s), lambda i, h: (i, 0, 0)),
        ],
        out_specs=pl.BlockSpec((1, s, HEAD_DIM), lambda i, h: (i, 0, h)),
        out_shape=jax.ShapeDtypeStruct((b, s, n_heads * HEAD_DIM), BF16),
        compiler_params=_params("arbitrary", "arbitrary"),
        name="fox_attention",
    )(qkv, qkv, qkv, cum)


Q_CHUNKS = 2
QB = Q_CHUNKS * CHUNK
PAD = LEFT_CHUNKS * CHUNK
WIN = (LEFT_CHUNKS + Q_CHUNKS) * CHUNK


def _chunk_kernel(q_ref, k_ref, v_ref, bias_ref, o_ref, kpad_ref, vpad_ref, *, scale):
    seq = q_ref.shape[1]
    kpad_ref[0:PAD, :] = jnp.zeros((PAD, HEAD_DIM), BF16)
    vpad_ref[0:PAD, :] = jnp.zeros((PAD, HEAD_DIM), BF16)
    kpad_ref[PAD:PAD + seq, :] = k_ref[0]
    vpad_ref[PAD:PAD + seq, :] = v_ref[0]
    col = lax.broadcasted_iota(jnp.int32, (QB, WIN), 1)

    def body(i, carry):
        q0 = pl.multiple_of(i * QB, QB)
        q = q_ref[0, pl.ds(q0, QB), :]
        k = kpad_ref[pl.ds(q0, WIN), :]
        v = vpad_ref[pl.ds(q0, WIN), :]
        s = lax.dot_general(q, k, (((1,), (1,)), ((), ())), preferred_element_type=F32) * scale
        s = s + bias_ref[0]
        s = jnp.where(col + q0 >= PAD, s, NEG_INF)
        m = jnp.max(s, axis=-1, keepdims=True)
        p = jnp.exp(s - m)
        l = jnp.sum(p, axis=-1, keepdims=True)
        out = jnp.dot(p.astype(BF16), v, preferred_element_type=F32)
        o_ref[0, pl.ds(q0, QB), :] = (out / l).astype(o_ref.dtype)
        return carry

    lax.fori_loop(0, seq // QB, body, 0)


def _chunk_bias(rel_table):
    max_rel = (rel_table.shape[1] - 1) // 2
    qi = np.arange(QB)[:, None]
    kj = np.arange(WIN)[None, :]
    rel_idx = np.clip(PAD + qi - kj, -max_rel, max_rel) + max_rel
    key_chunk, q_chunk = kj // CHUNK, qi // CHUNK
    visible = (key_chunk >= q_chunk) & (key_chunk <= q_chunk + LEFT_CHUNKS)
    bias = rel_table.astype(F32)[:, rel_idx]
    return jnp.where(visible[None], bias, NEG_INF)


def _chunk_attention(qkv, rel_table, col_blocks):
    b, s, _ = qkv.shape
    n_heads = rel_table.shape[0]
    assert s % QB == 0
    qc, kc, vc = col_blocks
    kern = functools.partial(_chunk_kernel, scale=HEAD_DIM ** -0.5)
    return pl.pallas_call(
        kern,
        grid=(b, n_heads),
        in_specs=[
            pl.BlockSpec((1, s, HEAD_DIM), lambda i, h: (i, 0, qc + h)),
            pl.BlockSpec((1, s, HEAD_DIM), lambda i, h: (i, 0, kc + h)),
            pl.BlockSpec((1, s, HEAD_DIM), lambda i, h: (i, 0, vc + h)),
            pl.BlockSpec((1, QB, WIN), lambda i, h: (h, 0, 0)),
        ],
        out_specs=pl.BlockSpec((1, s, HEAD_DIM), lambda i, h: (i, 0, h)),
        out_shape=jax.ShapeDtypeStruct((b, s, n_heads * HEAD_DIM), BF16),
        scratch_shapes=[pltpu.VMEM((PAD + s, HEAD_DIM), BF16), pltpu.VMEM((PAD + s, HEAD_DIM), BF16)],
        compiler_params=_params("arbitrary", "arbitrary"),
        name="chunk_attention",
    )(qkv, qkv, qkv, _chunk_bias(rel_table))


def _merge_kernel(h_ref, ya_ref, yb_ref, wga_ref, wgb_ref, wa_ref, wb_ref, bga_ref, bgb_ref, o_ref,
                  wga_s, wgb_s, wa_s, wb_s, *, rows_per_step):
    @pl.when(pl.program_id(1) == 0)
    def _():
        _cast_rows(wga_ref, wga_s, rows_per_step)
        _cast_rows(wgb_ref, wgb_s, rows_per_step)
        _cast_rows(wa_ref, wa_s, rows_per_step)
        _cast_rows(wb_ref, wb_s, rows_per_step)

    h = h_ref[...]
    gate_a = jax.nn.sigmoid(jnp.dot(h, wga_s[...], preferred_element_type=F32) + bga_ref[...])
    gate_b = jax.nn.sigmoid(jnp.dot(h, wgb_s[...], preferred_element_type=F32) + bgb_ref[...])
    proj_a = jnp.dot(ya_ref[...], wa_s[...], preferred_element_type=F32)
    proj_b = jnp.dot(yb_ref[...], wb_s[...], preferred_element_type=F32)
    o_ref[...] = (gate_a * proj_a + gate_b * proj_b).astype(o_ref.dtype)


def _gated_merge(h2d, ya2d, yb2d, layer, w_gate, b_gate, w_a, w_b):
    m, d = h2d.shape
    wa_rows, wb_rows = w_a.shape[1], w_b.shape[1]
    tn = _tile(d, 256)
    tm = _tile(m, 512, 8)
    nb = d // tn
    single = pl.Buffered(1)
    kern = functools.partial(_merge_kernel, rows_per_step=_tile(min(d, wa_rows, wb_rows), 256, 8))
    b2 = b_gate.reshape(1, 2 * d)
    return pl.pallas_call(
        kern,
        grid=(nb, m // tm),
        in_specs=[
            pl.BlockSpec((tm, d), lambda j, i: (i, 0)),
            pl.BlockSpec((tm, wa_rows), lambda j, i: (i, 0)),
            pl.BlockSpec((tm, wb_rows), lambda j, i: (i, 0)),
            pl.BlockSpec((None, d, tn), lambda j, i: (layer, 0, j), pipeline_mode=single),
            pl.BlockSpec((None, d, tn), lambda j, i: (layer, 0, nb + j), pipeline_mode=single),
            pl.BlockSpec((None, wa_rows, tn), lambda j, i: (layer, 0, j), pipeline_mode=single),
            pl.BlockSpec((None, wb_rows, tn), lambda j, i: (layer, 0, j), pipeline_mode=single),
            pl.BlockSpec((1, tn), lambda j, i: (0, j)),
            pl.BlockSpec((1, tn), lambda j, i: (0, nb + j)),
        ],
        out_specs=pl.BlockSpec((tm, tn), lambda j, i: (i, j)),
        out_shape=jax.ShapeDtypeStruct((m, d), BF16),
        scratch_shapes=[pltpu.VMEM((d, tn), BF16), pltpu.VMEM((d, tn), BF16),
                        pltpu.VMEM((wa_rows, tn), BF16), pltpu.VMEM((wb_rows, tn), BF16)],
        compiler_params=_params("arbitrary", "arbitrary"),
        name="gated_merge",
    )(h2d, ya2d, yb2d, w_gate, w_gate, w_a, w_b, b2, b2)


def _outproj_kernel(a_ref, w_ref, x_ref, gt_ref, o_ref, w_s, *, rows_per_step):
    @pl.when(pl.program_id(1) == 0)
    def _():
        _cast_rows(w_ref, w_s, rows_per_step)

    y = jnp.dot(a_ref[...], w_s[...], preferred_element_type=F32)
    o_ref[...] = x_ref[...] + gt_ref[0] * y


def _out_projection(a2d, w_out, layer, x2d, gate, seq):
    m, k = a2d.shape
    d = w_out.shape[2]
    b = gate.shape[0]
    tn = _tile(d, 512)
    tm = _tile(seq, 512, 8)
    steps_per_batch = seq // tm
    kern = functools.partial(_outproj_kernel, rows_per_step=_tile(k, 256, 8))
    return pl.pallas_call(
        kern,
        grid=(d // tn, m // tm),
        in_specs=[
            pl.BlockSpec((tm, k), lambda j, i: (i, 0)),
            pl.BlockSpec((None, k, tn), lambda j, i: (layer, 0, j), pipeline_mode=pl.Buffered(1)),
            pl.BlockSpec((tm, tn), lambda j, i: (i, j)),
            pl.BlockSpec((1, 1, tn), lambda j, i: (i // steps_per_batch, 0, j)),
        ],
        out_specs=pl.BlockSpec((tm, tn), lambda j, i: (i, j)),
        out_shape=jax.ShapeDtypeStruct((m, d), F32),
        scratch_shapes=[pltpu.VMEM((k, tn), BF16)],
        compiler_params=_params("arbitrary", "arbitrary"),
        name="out_projection",
    )(a2d, w_out, x2d, gate.reshape(b, 1, d))


INFO_E1, INFO_E2, INFO_W1, INFO_W2, INFO_R1, INFO_R2 = range(6)


def _router_kernel(h_ref, w_ref, b_ref, info_ref, cnt_ref, carry_ref, *, n_groups, e_per_group):
    @pl.when(pl.program_id(0) == 0)
    def _():
        carry_ref[...] = jnp.zeros_like(carry_ref)

    tm = h_ref.shape[0]
    logits = jnp.dot(h_ref[...].astype(BF16), w_ref[...].astype(BF16),
                     preferred_element_type=F32) + b_ref[...]
    lane = lax.broadcasted_iota(jnp.int32, (tm, LANES), 1).astype(F32)
    big = float(LANES)
    is_group = lane < n_groups
    gl = jnp.where(is_group, logits, -jnp.inf)
    g_max = jnp.max(gl, axis=-1, keepdims=True)
    g_sel = jnp.min(jnp.where(gl == g_max, lane, big), axis=-1, keepdims=True)
    g_w = 1.0 / jnp.sum(jnp.where(is_group, jnp.exp(gl - g_max), 0.0), axis=-1, keepdims=True)

    expert = lane - n_groups
    lo = g_sel * e_per_group
    in_group = jnp.logical_and(expert >= lo, expert < lo + e_per_group)
    el = jnp.where(in_group, logits, -jnp.inf)
    v1 = jnp.max(el, axis=-1, keepdims=True)
    e1 = jnp.min(jnp.where(el == v1, expert, big), axis=-1, keepdims=True)
    el2 = jnp.where(expert == e1, -jnp.inf, el)
    v2 = jnp.max(el2, axis=-1, keepdims=True)
    e2 = jnp.min(jnp.where(el2 == v2, expert, big), axis=-1, keepdims=True)
    t = jnp.exp(v2 - v1)
    w1 = g_w / (1.0 + t)
    w2 = g_w * t / (1.0 + t)

    onehot = jnp.where(jnp.logical_or(lane == e1, lane == e2), 1.0, 0.0)
    row = lax.broadcasted_iota(jnp.int32, (tm, tm), 0)
    col = lax.broadcasted_iota(jnp.int32, (tm, tm), 1)
    before = jnp.where(col < row, 1.0, 0.0).astype(BF16)
    prior = jnp.dot(before, onehot.astype(BF16), preferred_element_type=F32) + carry_ref[0:1, :]
    r1 = jnp.sum(jnp.where(lane == e1, prior, 0.0), axis=-1, keepdims=True)
    r2 = jnp.sum(jnp.where(lane == e2, prior, 0.0), axis=-1, keepdims=True)
    total = carry_ref[0:1, :] + jnp.sum(onehot, axis=0, keepdims=True)
    carry_ref[...] = jnp.broadcast_to(total, carry_ref.shape)
    cnt_ref[...] = jnp.broadcast_to(total, cnt_ref.shape)

    info = jnp.zeros((tm, LANES), F32)
    for idx, val in ((INFO_E1, e1), (INFO_E2, e2), (INFO_W1, w1), (INFO_W2, w2),
                     (INFO_R1, r1), (INFO_R2, r2)):
        info = jnp.where(lane == idx, val, info)
    info_ref[...] = info


def _router(h2d, w_rg, b_rg, w_re, b_re, e_per_group):
    m, d = h2d.shape
    n_groups = w_rg.shape[1]
    n_exp = w_re.shape[1]
    assert n_groups + n_exp <= LANES
    pad = LANES - n_groups - n_exp
    w = jnp.pad(jnp.concatenate([w_rg, w_re], axis=1), ((0, 0), (0, pad)))
    bias = jnp.pad(jnp.concatenate([b_rg, b_re]).astype(F32), (0, pad)).reshape(1, LANES)
    tm = _tile(m, 256, 8)
    kern = functools.partial(_router_kernel, n_groups=n_groups, e_per_group=e_per_group)
    return pl.pallas_call(
        kern,
        grid=(m // tm,),
        in_specs=[
            pl.BlockSpec((tm, d), lambda i: (i, 0)),
            pl.BlockSpec((d, LANES), lambda i: (0, 0)),
            pl.BlockSpec((1, LANES), lambda i: (0, 0)),
        ],
        out_specs=[pl.BlockSpec((tm, LANES), lambda i: (i, 0)),
                   pl.BlockSpec((8, LANES), lambda i: (0, 0))],
        out_shape=[jax.ShapeDtypeStruct((m, LANES), F32), jax.ShapeDtypeStruct((8, LANES), F32)],
        scratch_shapes=[pltpu.VMEM((8, LANES), F32)],
        compiler_params=_params("arbitrary"),
        name="moe_router",
    )(h2d, w, bias)


def _gather_rows(idx_ref, base, n_rows, src_hbm, dst_ref, sem):
    def body(r, carry):
        row = idx_ref[base + r]
        pltpu.make_async_copy(src_hbm.at[pl.ds(row, 1)], dst_ref.at[pl.ds(r, 1)], sem).start()
        return carry

    lax.fori_loop(0, n_rows, body, 0)


def _wait_rows(n_rows, src_hbm, dst_ref, sem):
    pltpu.make_async_copy(src_hbm.at[pl.ds(0, n_rows)], dst_ref, sem).wait()


def _expert_up_kernel(te_ref, nt_ref, tok_ref, h_hbm, wg_ref, wu_ref, a_ref, xbuf, sem, *, tm):
    i = pl.program_id(0)
    n_tiles = nt_ref[0]

    @pl.when(i == 0)
    def _():
        _gather_rows(tok_ref, 0, tm, h_hbm, xbuf.at[0], sem.at[0])

    @pl.when(i + 1 < n_tiles)
    def _():
        nxt = (i + 1) % 2
        _gather_rows(tok_ref, (i + 1) * tm, tm, h_hbm, xbuf.at[nxt], sem.at[nxt])

    @pl.when(i < n_tiles)
    def _():
        slot = i % 2
        _wait_rows(tm, h_hbm, xbuf.at[slot], sem.at[slot])
        x = xbuf[slot].astype(BF16)
        g = jnp.dot(x, wg_ref[0].astype(BF16), preferred_element_type=F32)
        u = jnp.dot(x, wu_ref[0].astype(BF16), preferred_element_type=F32)
        a_ref[...] = (g * jax.nn.sigmoid(g) * u).astype(a_ref.dtype)

    @pl.when(i >= n_tiles)
    def _():
        a_ref[...] = jnp.zeros_like(a_ref)


def _expert_down_kernel(te_ref, nt_ref, a_ref, wd_ref, y_ref):
    i = pl.program_id(0)

    @pl.when(i < nt_ref[0])
    def _():
        y_ref[...] = jnp.dot(a_ref[...], wd_ref[0].astype(BF16), preferred_element_type=F32)

    @pl.when(i >= nt_ref[0])
    def _():
        y_ref[...] = jnp.zeros_like(y_ref)


def _experts(h2d, tile_expert, n_tiles, sorted_token, w_gate, w_up, w_down, tm, max_tiles):
    m, d = h2d.shape
    n_exp, _, f = w_gate.shape
    a = pl.pallas_call(
        functools.partial(_expert_up_kernel, tm=tm),
        grid_spec=pltpu.PrefetchScalarGridSpec(
            num_scalar_prefetch=3,
            grid=(max_tiles,),
            in_specs=[
                pl.BlockSpec(memory_space=pl.ANY),
                pl.BlockSpec((1, d, f), lambda i, te, nt, tok: (te[i], 0, 0)),
                pl.BlockSpec((1, d, f), lambda i, te, nt, tok: (te[i], 0, 0)),
            ],
            out_specs=pl.BlockSpec((tm, f), lambda i, te, nt, tok: (i, 0)),
            scratch_shapes=[pltpu.VMEM((2, tm, d), F32), pltpu.SemaphoreType.DMA((2,))],
        ),
        out_shape=jax.ShapeDtypeStruct((max_tiles * tm, f), BF16),
        compiler_params=_params("arbitrary"),
        name="moe_expert_up",
    )(tile_expert, n_tiles, sorted_token, h2d, w_gate, w_up)
    return pl.pallas_call(
        _expert_down_kernel,
        grid_spec=pltpu.PrefetchScalarGridSpec(
            num_scalar_prefetch=2,
            grid=(max_tiles,),
            in_specs=[
                pl.BlockSpec((tm, f), lambda i, te, nt: (i, 0)),
                pl.BlockSpec((1, f, d), lambda i, te, nt: (te[i], 0, 0)),
            ],
            out_specs=pl.BlockSpec((tm, d), lambda i, te, nt: (i, 0)),
        ),
        out_shape=jax.ShapeDtypeStruct((max_tiles * tm, d), F32),
        compiler_params=_params("arbitrary"),
        name="moe_expert_down",
    )(tile_expert, n_tiles, a, w_down)


def _combine_kernel(p1_ref, p2_ref, y_hbm, x_ref, info_ref, gt_ref, o_ref, ybuf, sem, *, tm):
    i = pl.program_id(0)
    n_steps = pl.num_programs(0)

    def gather(step, slot):
        _gather_rows(p1_ref, step * tm, tm, y_hbm, ybuf.at[slot, 0], sem.at[slot])
        _gather_rows(p2_ref, step * tm, tm, y_hbm, ybuf.at[slot, 1], sem.at[slot])

    @pl.when(i == 0)
    def _():
        gather(0, 0)

    @pl.when(i + 1 < n_steps)
    def _():
        gather(i + 1, (i + 1) % 2)

    slot = i % 2
    _wait_rows(tm, y_hbm, ybuf.at[slot, 0], sem.at[slot])
    _wait_rows(tm, y_hbm, ybuf.at[slot, 1], sem.at[slot])
    info = info_ref[...]
    w1 = info[:, INFO_W1:INFO_W1 + 1]
    w2 = info[:, INFO_W2:INFO_W2 + 1]
    y = w1 * ybuf[slot, 0] + w2 * ybuf[slot, 1]
    o_ref[...] = x_ref[...] + gt_ref[0] * y


def _moe_combine(y_sorted, pos1, pos2, x2d, info, gate, seq):
    m, d = x2d.shape
    b = gate.shape[0]
    tm = _tile(seq, 128, 8)
    steps_per_batch = seq // tm
    return pl.pallas_call(
        functools.partial(_combine_kernel, tm=tm),
        grid_spec=pltpu.PrefetchScalarGridSpec(
            num_scalar_prefetch=2,
            grid=(m // tm,),
            in_specs=[
                pl.BlockSpec(memory_space=pl.ANY),
                pl.BlockSpec((tm, d), lambda i, p1, p2: (i, 0)),
                pl.BlockSpec((tm, LANES), lambda i, p1, p2: (i, 0)),
                pl.BlockSpec((1, 1, d), lambda i, p1, p2: (i // steps_per_batch, 0, 0)),
            ],
            out_specs=pl.BlockSpec((tm, d), lambda i, p1, p2: (i, 0)),
            scratch_shapes=[pltpu.VMEM((2, 2, tm, d), F32), pltpu.SemaphoreType.DMA((2,))],
        ),
        out_shape=jax.ShapeDtypeStruct((m, d), F32),
        compiler_params=_params("arbitrary"),
        name="moe_combine",
    )(pos1, pos2, y_sorted, x2d, info, gate.reshape(b, 1, d))


EXPERT_TILE = 256


def _hier_moe(h2d, x2d, gate, seq, layer, w_rg, b_rg, w_re, b_re, w_eg, w_eu, w_ed):
    m, d = h2d.shape
    n_layers, n_groups, e_per_group, _, f = w_eg.shape
    n_exp = n_groups * e_per_group
    tm = EXPERT_TILE
    max_tiles = (2 * m) // tm + n_exp
    info, cnt = _router(h2d, w_rg, b_rg, w_re, b_re, e_per_group)

    counts = cnt[0, :n_exp].astype(jnp.int32)
    tiles_per_expert = (counts + tm - 1) // tm
    tile_end = jnp.cumsum(tiles_per_expert)
    tile_start = tile_end - tiles_per_expert
    n_tiles = tile_end[-1:]
    expert = info[:, INFO_E1:INFO_E2 + 1].astype(jnp.int32)
    rank = info[:, INFO_R1:INFO_R2 + 1].astype(jnp.int32)
    pos = tile_start[expert] * tm + rank
    tile_ids = jnp.arange(max_tiles, dtype=jnp.int32)
    tile_expert = jnp.searchsorted(tile_end, tile_ids, side="right").astype(jnp.int32)
    last_expert = tile_expert[jnp.maximum(n_tiles[0] - 1, 0)]
    tile_expert = jnp.where(tile_ids < n_tiles[0], tile_expert, last_expert) + layer * n_exp
    token = jnp.broadcast_to(jnp.arange(m, dtype=jnp.int32)[:, None], (m, 2))
    sorted_token = jnp.zeros((max_tiles * tm,), jnp.int32).at[pos.reshape(-1)].set(token.reshape(-1))

    y_sorted = _experts(h2d, tile_expert, n_tiles, sorted_token,
                        w_eg.reshape(n_layers * n_exp, d, f), w_eu.reshape(n_layers * n_exp, d, f),
                        w_ed.reshape(n_layers * n_exp, f, d), tm, max_tiles)
    return _moe_combine(y_sorted, pos[:, 0], pos[:, 1], x2d, info, gate, seq)


def kernel(x, c, ada_w, ada_b, norm_mix_g, norm_ffn_g, w_in, b_forget, rel_bias, w_branch_a, w_branch_b, w_gate, b_gate, w_out, w_router_group, b_router_group, w_router_expert, b_router_expert, w_exp_gate, w_exp_up, w_exp_down, final_norm_g):
    b, s, d = x.shape
    n_layers = ada_w.shape[0]
    m = b * s
    heads_a, heads_b = b_forget.shape[1], rel_bias.shape[1]
    wa3 = 3 * heads_a * HEAD_DIM
    assert w_branch_a.shape[1] == heads_a * HEAD_DIM and w_branch_b.shape[1] == heads_b * HEAD_DIM
    assert w_in.shape[2] == wa3 + heads_a + 3 * heads_b * HEAD_DIM

    mod = _adaln_mod(c, ada_w, ada_b)
    for l in range(n_layers):
        sh_m, sc_m, gt_m, sh_f, sc_f, gt_f = [mod[l, :, i * d:(i + 1) * d] for i in range(6)]
        h = _norm_mod(x, norm_mix_g[l], sh_m, sc_m, BF16)
        h2d = h.reshape(m, d)
        qkv = _in_projection(h2d, w_in, l, wa3, heads_a).reshape(b, s, -1)
        cum = _forget_cumsum(h, w_in, l, b_forget[l], wa3)
        ya = _fox_attention(qkv, cum, heads_a, (0, heads_a, 2 * heads_a))
        off = 3 * heads_a
        yb = _chunk_attention(qkv, rel_bias[l], (off, off + heads_b, off + 2 * heads_b))
        merged = _gated_merge(h2d, ya.reshape(m, -1), yb.reshape(m, -1), l, w_gate, b_gate[l],
                              w_branch_a, w_branch_b)
        x2d = _out_projection(merged, w_out, l, x.reshape(m, d), gt_m, s)
        h_ffn = _norm_mod(x2d.reshape(b, s, d), norm_ffn_g[l], sh_f, sc_f, F32).reshape(m, d)
        x2d = _hier_moe(h_ffn, x2d, gt_f, s, l, w_router_group[l], b_router_group[l],
                        w_router_expert[l], b_router_expert[l],
                        w_exp_gate, w_exp_up, w_exp_down)
        x = x2d.reshape(b, s, d)
    zeros = jnp.zeros((b, d), F32)
    return _norm_mod(x, final_norm_g, zeros, zeros, x.dtype)
```

```python
import functools

import jax
import jax.numpy as jnp
from jax import lax
from jax.experimental import pallas as pl
from jax.experimental.pallas import tpu as pltpu

CHUNK = 64
LEFT_CHUNKS = 8
EPS = 1e-6
NEG_INF = -1e30
LANES = 128
HEAD_DIM = 128
LOG2E = 1.4426950408889634
V7X_VMEM_LIMIT_BYTES = 60000 * 1024

F32 = jnp.float32
BF16 = jnp.bfloat16


def _params(*sem):
    return pltpu.CompilerParams(dimension_semantics=sem, vmem_limit_bytes=V7X_VMEM_LIMIT_BYTES)


def _tile(n, pref, mult=LANES):
    t = (min(pref, n) // mult) * mult
    while t >= mult:
        if n % t == 0:
            return t
        t -= mult
    return n


def _cast_rows(src_ref, dst_ref, rows_per_step):
    n = src_ref.shape[0]

    def body(r, carry):
        rows = pl.ds(pl.multiple_of(r * rows_per_step, rows_per_step), rows_per_step)
        dst_ref[rows, :] = src_ref[rows, :].astype(dst_ref.dtype)
        return carry

    lax.fori_loop(0, n // rows_per_step, body, 0)


def _mod_kernel(c_ref, w_ref, b_ref, o_ref):
    c = c_ref[...]
    c_act = (c * jax.nn.sigmoid(c)).astype(BF16)
    o_ref[0] = jnp.dot(c_act, w_ref[0].astype(BF16), preferred_element_type=F32) + b_ref[0]


def _adaln_mod(c, ada_w, ada_b):
    n_layers, d, n = ada_w.shape
    b = c.shape[0]
    bp = -(-b // 8) * 8
    c_pad = jnp.pad(c, ((0, bp - b), (0, 0)))
    tn = _tile(n, 512)
    out = pl.pallas_call(
        _mod_kernel,
        grid=(n_layers, n // tn),
        in_specs=[
            pl.BlockSpec((bp, d), lambda l, j: (0, 0)),
            pl.BlockSpec((1, d, tn), lambda l, j: (l, 0, j)),
            pl.BlockSpec((1, 1, tn), lambda l, j: (l, 0, j)),
        ],
        out_specs=pl.BlockSpec((1, bp, tn), lambda l, j: (l, 0, j)),
        out_shape=jax.ShapeDtypeStruct((n_layers, bp, n), F32),
        compiler_params=_params("arbitrary", "arbitrary"),
        name="adaln_mod",
    )(c_pad, ada_w, ada_b.reshape(n_layers, 1, n))
    return out[:, :b]


def _norm_mod_kernel(x_ref, g_ref, sh_ref, sc_ref, o_ref):
    x = x_ref[0]
    ms = jnp.mean(x * x, axis=-1, keepdims=True)
    y = x * lax.rsqrt(ms + EPS) * g_ref[...]
    o_ref[0] = (y * (1.0 + sc_ref[0]) + sh_ref[0]).astype(o_ref.dtype)


def _norm_mod(x, g, shift, scale, out_dtype):
    b, s, d = x.shape
    ts = _tile(s, 256, 8)
    return pl.pallas_call(
        _norm_mod_kernel,
        grid=(b, s // ts),
        in_specs=[
            pl.BlockSpec((1, ts, d), lambda i, j: (i, j, 0)),
            pl.BlockSpec((1, d), lambda i, j: (0, 0)),
            pl.BlockSpec((1, 1, d), lambda i, j: (i, 0, 0)),
            pl.BlockSpec((1, 1, d), lambda i, j: (i, 0, 0)),
        ],
        out_specs=pl.BlockSpec((1, ts, d), lambda i, j: (i, j, 0)),
        out_shape=jax.ShapeDtypeStruct((b, s, d), out_dtype),
        compiler_params=_params("arbitrary", "arbitrary"),
        name="norm_mod",
    )(x, g.reshape(1, d), shift.reshape(b, 1, d), scale.reshape(b, 1, d))


def _dot_nt(a, b):
    return lax.dot_general(a, b, (((1,), (1,)), ((), ())), preferred_element_type=F32)


def _inproj_kernel(h_ref, w_ref, o_ref, wbf_ref, *, rows_per_step):
    @pl.when(pl.program_id(1) == 0)
    def _():
        _cast_rows(w_ref, wbf_ref, rows_per_step)

    o_ref[...] = _dot_nt(h_ref[...], wbf_ref[...]).astype(o_ref.dtype)


def _in_projection(h2d, w_in_t, layer, w_a3, n_forget):
    m, d = h2d.shape
    n_out = w_in_t.shape[1] - n_forget
    tn = _tile(w_a3, 512)
    assert n_out % tn == 0 and w_a3 % tn == 0 and n_forget % 8 == 0
    tm = _tile(m, 1024, 8)
    n_plain = w_a3 // tn
    kern = functools.partial(_inproj_kernel, rows_per_step=_tile(tn, 128, 8))
    return pl.pallas_call(
        kern,
        grid=(n_out // tn, m // tm),
        in_specs=[
            pl.BlockSpec((tm, d), lambda j, i: (i, 0)),
            pl.BlockSpec((None, pl.Element(tn), pl.Element(d)),
                         lambda j, i: (layer, pl.multiple_of(
                             j * tn + jnp.where(j >= n_plain, n_forget, 0), 8), 0)),
        ],
        out_specs=pl.BlockSpec((tm, tn), lambda j, i: (i, j)),
        out_shape=jax.ShapeDtypeStruct((m, n_out), BF16),
        scratch_shapes=[pltpu.VMEM((tn, d), BF16)],
        compiler_params=_params("arbitrary", "arbitrary"),
        name="in_projection",
    )(h2d, w_in_t)


def _split3_dot(x, u):
    x1 = x.astype(BF16)
    r1 = x - x1.astype(F32)
    x2 = r1.astype(BF16)
    x3 = (r1 - x2.astype(F32)).astype(BF16)
    ub = u.astype(BF16)
    return (jnp.dot(x1, ub, preferred_element_type=F32)
            + jnp.dot(x2, ub, preferred_element_type=F32)
            + jnp.dot(x3, ub, preferred_element_type=F32))


def _forget_kernel(h_ref, w_ref, b_ref, o_ref, carry_ref):
    s = pl.program_id(1)

    @pl.when(s == 0)
    def _():
        carry_ref[...] = jnp.zeros_like(carry_ref)

    ts = h_ref.shape[1]
    f = _dot_nt(w_ref[...].astype(BF16), h_ref[0]) + b_ref[...]
    log_f = jnp.minimum(f, 0.0) - jnp.log1p(jnp.exp(-jnp.abs(f)))
    row = lax.broadcasted_iota(jnp.int32, (ts, ts), 0)
    col = lax.broadcasted_iota(jnp.int32, (ts, ts), 1)
    upper = jnp.where(row <= col, 1.0, 0.0).astype(F32)
    cs = _split3_dot(log_f, upper) + carry_ref[:, 0:1]
    o_ref[0] = cs * LOG2E
    carry_ref[...] = jnp.broadcast_to(cs[:, ts - 1:ts], carry_ref.shape)


def _forget_cumsum(h, w_in_t, layer, b_forget, row0):
    b, s, d = h.shape
    n_heads = b_forget.shape[0]
    hp = -(-n_heads // 8) * 8
    assert row0 % 8 == 0
    ts = _tile(s, 512)
    bias = jnp.pad(b_forget.astype(F32), (0, hp - n_heads)).reshape(hp, 1)
    return pl.pallas_call(
        _forget_kernel,
        grid=(b, s // ts),
        in_specs=[
            pl.BlockSpec((1, ts, d), lambda i, j: (i, j, 0)),
            pl.BlockSpec((None, pl.Element(hp), pl.Element(d)), lambda i, j: (layer, row0, 0)),
            pl.BlockSpec((hp, 1), lambda i, j: (0, 0)),
        ],
        out_specs=pl.BlockSpec((1, hp, ts), lambda i, j: (i, 0, j)),
        out_shape=jax.ShapeDtypeStruct((b, hp, s), F32),
        scratch_shapes=[pltpu.VMEM((hp, LANES), F32)],
        compiler_params=_params("arbitrary", "arbitrary"),
        name="forget_cumsum",
    )(h, w_in_t, bias)


def _fox_kernel(q_ref, k_ref, v_ref, cum_ref, o_ref, *, tq, scale_log2e):
    head = pl.program_id(1)
    seq = q_ref.shape[1]
    row = lax.broadcasted_iota(jnp.int32, (tq, tq), 0)
    col = lax.broadcasted_iota(jnp.int32, (tq, tq), 1)
    causal = row >= col
    decay = cum_ref[0, pl.ds(head, 1), :]

    n_blocks = seq // tq

    def scores(iq):
        q0 = iq * tq
        q = q_ref[0, q0:q0 + tq, :]
        s_diag = _dot_nt(q, k_ref[0, q0:q0 + tq, :]) * scale_log2e - decay[:, q0:q0 + tq]
        s_diag = jnp.where(causal, s_diag, NEG_INF)
        s_past = None
        if iq > 0:
            s_past = _dot_nt(q, k_ref[0, 0:q0, :]) * scale_log2e - decay[:, 0:q0]
        return s_diag, s_past

    nxt = scores(0)
    for iq in range(n_blocks):
        q0 = iq * tq
        s_diag, s_past = nxt
        if iq + 1 < n_blocks:
            nxt = scores(iq + 1)
        m = jnp.max(s_diag, axis=-1, keepdims=True)
        if iq > 0:
            m = jnp.maximum(m, jnp.max(s_past, axis=-1, keepdims=True))
        p_diag = jnp.exp2(s_diag - m)
        l = jnp.sum(p_diag, axis=-1, keepdims=True)
        acc = jnp.dot(p_diag.astype(BF16), v_ref[0, q0:q0 + tq, :], preferred_element_type=F32)
        if iq > 0:
            p_past = jnp.exp2(s_past - m)
            l = l + jnp.sum(p_past, axis=-1, keepdims=True)
            acc = acc + jnp.dot(p_past.astype(BF16), v_ref[0, 0:q0, :], preferred_element_type=F32)
        o_ref[0, q0:q0 + tq, :] = (acc / l).astype(o_ref.dtype)


def _fox_attention(qkv, cum, n_heads, col_blocks):
    b, s, _ = qkv.shape
    hp = cum.shape[1]
    tq = _tile(s, 256)
    qc, kc, vc = col_blocks
    kern = functools.partial(_fox_kernel, tq=tq, scale_log2e=HEAD_DIM ** -0.5 * LOG2E)
    return pl.pallas_call(
        kern,
        grid=(b, n_heads),
        in_specs=[
            pl.BlockSpec((1, s, HEAD_DIM), lambda i, h: (i, 0, qc + h)),
            pl.BlockSpec((1, s, HEAD_DIM), lambda i, h: (i, 0, kc + h)),
            pl.BlockSpec((1, s, HEAD_DIM), lambda i, h: (i, 0, vc + h)),
            pl.BlockSpec((1, hp, s), lambda i, h: (i, 0, 0)),
        ],
        out_specs=pl.BlockSpec((1, s, HEAD_DIM), lambda i, h: (i, 0, h)),
        out_shape=jax.ShapeDtypeStruct((b, s, n_heads * HEAD_DIM), BF16),
        compiler_params=_params("arbitrary", "arbitrary"),
        name="fox_attention",
    )(qkv, qkv, qkv, cum)


Q_CHUNKS = 2
QB = Q_CHUNKS * CHUNK
PAD = LEFT_CHUNKS * CHUNK
WIN = (LEFT_CHUNKS + Q_CHUNKS) * CHUNK
CHUNK_SHIFT = CHUNK.bit_length() - 1


def _chunk_bias_kernel(tab_ref, o_ref, *, max_rel, c0):
    h = pl.program_id(0)

    def visible(qi, kj):
        q_chunk = jnp.right_shift(qi, CHUNK_SHIFT)
        k_chunk = jnp.right_shift(kj, CHUNK_SHIFT)
        return jnp.logical_and(k_chunk >= q_chunk, k_chunk <= q_chunk + LEFT_CHUNKS)

    if c0 > 0:
        qi = lax.broadcasted_iota(jnp.int32, (QB, c0), 0)
        kj = lax.broadcasted_iota(jnp.int32, (QB, c0), 1)
        far = tab_ref[h, 2 * max_rel] * LOG2E
        o_ref[0, :, 0:c0] = jnp.where(visible(qi, kj), far, NEG_INF)
    qi = lax.broadcasted_iota(jnp.int32, (QB, WIN - c0), 0)
    kj = lax.broadcasted_iota(jnp.int32, (QB, WIN - c0), 1) + c0
    idx = jnp.clip(PAD + qi - kj, -max_rel, max_rel) + max_rel

    def body(r, acc):
        return jnp.where(idx == r, tab_ref[h, r], acc)

    near = lax.fori_loop(0, 2 * max_rel + 1, body, jnp.zeros((QB, WIN - c0), F32))
    o_ref[0, :, c0:WIN] = jnp.where(visible(qi, kj), near * LOG2E, NEG_INF)


def _chunk_bias(rel_table):
    n_heads, n_rel = rel_table.shape
    max_rel = (n_rel - 1) // 2
    c0 = max(0, (PAD - max_rel + 1) // LANES * LANES)
    return pl.pallas_call(
        functools.partial(_chunk_bias_kernel, max_rel=max_rel, c0=c0),
        grid=(n_heads,),
        in_specs=[pl.BlockSpec(memory_space=pltpu.SMEM)],
        out_specs=pl.BlockSpec((1, QB, WIN), lambda h: (h, 0, 0)),
        out_shape=jax.ShapeDtypeStruct((n_heads, QB, WIN), F32),
        compiler_params=_params("arbitrary"),
        name="chunk_bias",
    )(rel_table.astype(F32))


def _chunk_kernel(q_ref, k_ref, v_ref, bias_ref, o_ref, *, scale_log2e):
    seq = q_ref.shape[1]
    n_blocks = seq // QB

    def scores(i):
        q0 = i * QB
        k_lo = max(0, q0 - PAD)
        width = q0 + QB - k_lo
        q = q_ref[0, q0:q0 + QB, :]
        return _dot_nt(q, k_ref[0, k_lo:q0 + QB, :]) * scale_log2e + bias_ref[0, :, WIN - width:WIN]

    s_next = scores(0)
    for i in range(n_blocks):
        q0 = i * QB
        k_lo = max(0, q0 - PAD)
        s = s_next
        if i + 1 < n_blocks:
            s_next = scores(i + 1)
        m = jnp.max(s, axis=-1, keepdims=True)
        p = jnp.exp2(s - m)
        l = jnp.sum(p, axis=-1, keepdims=True)
        out = jnp.dot(p.astype(BF16), v_ref[0, k_lo:q0 + QB, :], preferred_element_type=F32)
        o_ref[0, q0:q0 + QB, :] = (out / l).astype(o_ref.dtype)


def _chunk_attention(qkv, rel_table, col_blocks):
    b, s, _ = qkv.shape
    n_heads = rel_table.shape[0]
    assert s % QB == 0
    qc, kc, vc = col_blocks
    kern = functools.partial(_chunk_kernel, scale_log2e=HEAD_DIM ** -0.5 * LOG2E)
    return pl.pallas_call(
        kern,
        grid=(b, n_heads),
        in_specs=[
            pl.BlockSpec((1, s, HEAD_DIM), lambda i, h: (i, 0, qc + h)),
            pl.BlockSpec((1, s, HEAD_DIM), lambda i, h: (i, 0, kc + h)),
            pl.BlockSpec((1, s, HEAD_DIM), lambda i, h: (i, 0, vc + h)),
            pl.BlockSpec((1, QB, WIN), lambda i, h: (h, 0, 0)),
        ],
        out_specs=pl.BlockSpec((1, s, HEAD_DIM), lambda i, h: (i, 0, h)),
        out_shape=jax.ShapeDtypeStruct((b, s, n_heads * HEAD_DIM), BF16),
        compiler_params=_params("arbitrary", "arbitrary"),
        name="chunk_attention",
    )(qkv, qkv, qkv, _chunk_bias(rel_table))


def _merge_kernel(h_ref, ya_ref, yb_ref, wga_ref, wgb_ref, wa_ref, wb_ref, bga_ref, bgb_ref, o_ref,
                  wga_s, wgb_s, wa_s, wb_s, *, rows_per_step):
    @pl.when(pl.program_id(1) == 0)
    def _():
        _cast_rows(wga_ref, wga_s, rows_per_step)
        _cast_rows(wgb_ref, wgb_s, rows_per_step)
        _cast_rows(wa_ref, wa_s, rows_per_step)
        _cast_rows(wb_ref, wb_s, rows_per_step)

    h = h_ref[...]
    gate_a = jax.nn.sigmoid(jnp.dot(h, wga_s[...], preferred_element_type=F32) + bga_ref[...])
    gate_b = jax.nn.sigmoid(jnp.dot(h, wgb_s[...], preferred_element_type=F32) + bgb_ref[...])
    proj_a = jnp.dot(ya_ref[...], wa_s[...], preferred_element_type=F32)
    proj_b = jnp.dot(yb_ref[...], wb_s[...], preferred_element_type=F32)
    o_ref[...] = (gate_a * proj_a + gate_b * proj_b).astype(o_ref.dtype)


def _gated_merge(h2d, ya2d, yb2d, layer, w_gate, b_gate, w_a, w_b):
    m, d = h2d.shape
    wa_rows, wb_rows = w_a.shape[1], w_b.shape[1]
    tn = _tile(d, 256)
    tm = _tile(m, 1024, 8)
    nb = d // tn
    single = pl.Buffered(1)
    kern = functools.partial(_merge_kernel, rows_per_step=_tile(min(d, wa_rows, wb_rows), 256, 8))
    b2 = b_gate.reshape(1, 2 * d)
    return pl.pallas_call(
        kern,
        grid=(nb, m // tm),
        in_specs=[
            pl.BlockSpec((tm, d), lambda j, i: (i, 0)),
            pl.BlockSpec((tm, wa_rows), lambda j, i: (i, 0)),
            pl.BlockSpec((tm, wb_rows), lambda j, i: (i, 0)),
            pl.BlockSpec((None, d, tn), lambda j, i: (layer, 0, j), pipeline_mode=single),
            pl.BlockSpec((None, d, tn), lambda j, i: (layer, 0, nb + j), pipeline_mode=single),
            pl.BlockSpec((None, wa_rows, tn), lambda j, i: (layer, 0, j), pipeline_mode=single),
            pl.BlockSpec((None, wb_rows, tn), lambda j, i: (layer, 0, j), pipeline_mode=single),
            pl.BlockSpec((1, tn), lambda j, i: (0, j)),
            pl.BlockSpec((1, tn), lambda j, i: (0, nb + j)),
        ],
        out_specs=pl.BlockSpec((tm, tn), lambda j, i: (i, j)),
        out_shape=jax.ShapeDtypeStruct((m, d), BF16),
        scratch_shapes=[pltpu.VMEM((d, tn), BF16), pltpu.VMEM((d, tn), BF16),
                        pltpu.VMEM((wa_rows, tn), BF16), pltpu.VMEM((wb_rows, tn), BF16)],
        compiler_params=_params("arbitrary", "arbitrary"),
        name="gated_merge",
    )(h2d, ya2d, yb2d, w_gate, w_gate, w_a, w_b, b2, b2)


def _outproj_kernel(a_ref, w_ref, x_ref, gt_ref, o_ref, w_s, *, rows_per_step):
    @pl.when(pl.program_id(1) == 0)
    def _():
        _cast_rows(w_ref, w_s, rows_per_step)

    y = jnp.dot(a_ref[...], w_s[...], preferred_element_type=F32)
    o_ref[...] = x_ref[...] + gt_ref[0] * y


def _out_projection(a2d, w_out, layer, x2d, gate, seq):
    m, k = a2d.shape
    d = w_out.shape[2]
    b = gate.shape[0]
    tn = _tile(d, 512)
    tm = _tile(seq, 1024, 8)
    steps_per_batch = seq // tm
    kern = functools.partial(_outproj_kernel, rows_per_step=_tile(k, 256, 8))
    return pl.pallas_call(
        kern,
        grid=(d // tn, m // tm),
        in_specs=[
            pl.BlockSpec((tm, k), lambda j, i: (i, 0)),
            pl.BlockSpec((None, k, tn), lambda j, i: (layer, 0, j), pipeline_mode=pl.Buffered(1)),
            pl.BlockSpec((tm, tn), lambda j, i: (i, j)),
            pl.BlockSpec((1, 1, tn), lambda j, i: (i // steps_per_batch, 0, j)),
        ],
        out_specs=pl.BlockSpec((tm, tn), lambda j, i: (i, j)),
        out_shape=jax.ShapeDtypeStruct((m, d), F32),
        scratch_shapes=[pltpu.VMEM((k, tn), BF16)],
        compiler_params=_params("arbitrary", "arbitrary"),
        name="out_projection",
    )(a2d, w_out, x2d, gate.reshape(b, 1, d))


INFO_E1, INFO_E2, INFO_W1, INFO_W2, INFO_R1, INFO_R2 = range(6)


def _router_kernel(h_ref, w_ref, b_ref, info_ref, cnt_ref, carry_ref, *, n_groups, e_per_group):
    @pl.when(pl.program_id(0) == 0)
    def _():
        carry_ref[...] = jnp.zeros_like(carry_ref)

    tm = h_ref.shape[0]
    logits = jnp.dot(h_ref[...].astype(BF16), w_ref[...].astype(BF16),
                     preferred_element_type=F32) + b_ref[...]
    lane = lax.broadcasted_iota(jnp.int32, (tm, LANES), 1).astype(F32)
    big = float(LANES)
    is_group = lane < n_groups
    gl = jnp.where(is_group, logits, -jnp.inf)
    g_max = jnp.max(gl, axis=-1, keepdims=True)
    g_sel = jnp.min(jnp.where(gl == g_max, lane, big), axis=-1, keepdims=True)
    g_w = 1.0 / jnp.sum(jnp.where(is_group, jnp.exp(gl - g_max), 0.0), axis=-1, keepdims=True)

    expert = lane - n_groups
    lo = g_sel * e_per_group
    in_group = jnp.logical_and(expert >= lo, expert < lo + e_per_group)
    el = jnp.where(in_group, logits, -jnp.inf)
    v1 = jnp.max(el, axis=-1, keepdims=True)
    e1 = jnp.min(jnp.where(el == v1, expert, big), axis=-1, keepdims=True)
    el2 = jnp.where(expert == e1, -jnp.inf, el)
    v2 = jnp.max(el2, axis=-1, keepdims=True)
    e2 = jnp.min(jnp.where(el2 == v2, expert, big), axis=-1, keepdims=True)
    t = jnp.exp(v2 - v1)
    w1 = g_w / (1.0 + t)
    w2 = g_w * t / (1.0 + t)

    onehot = jnp.where(jnp.logical_or(lane == e1, lane == e2), 1.0, 0.0)
    row = lax.broadcasted_iota(jnp.int32, (tm, tm), 0)
    col = lax.broadcasted_iota(jnp.int32, (tm, tm), 1)
    before = jnp.where(col < row, 1.0, 0.0).astype(BF16)
    prior = jnp.dot(before, onehot.astype(BF16), preferred_element_type=F32) + carry_ref[0:1, :]
    r1 = jnp.sum(jnp.where(lane == e1, prior, 0.0), axis=-1, keepdims=True)
    r2 = jnp.sum(jnp.where(lane == e2, prior, 0.0), axis=-1, keepdims=True)
    total = carry_ref[0:1, :] + jnp.sum(onehot, axis=0, keepdims=True)
    carry_ref[...] = jnp.broadcast_to(total, carry_ref.shape)
    cnt_ref[...] = jnp.broadcast_to(total, cnt_ref.shape)

    info = jnp.zeros((tm, LANES), F32)
    for idx, val in ((INFO_E1, e1), (INFO_E2, e2), (INFO_W1, w1), (INFO_W2, w2),
                     (INFO_R1, r1), (INFO_R2, r2)):
        info = jnp.where(lane == idx, val, info)
    info_ref[...] = info


def _router(h2d, w_rg, b_rg, w_re, b_re, e_per_group):
    m, d = h2d.shape
    n_groups = w_rg.shape[1]
    n_exp = w_re.shape[1]
    assert n_groups + n_exp <= LANES
    pad = LANES - n_groups - n_exp
    w = jnp.pad(jnp.concatenate([w_rg, w_re], axis=1), ((0, 0), (0, pad)))
    bias = jnp.pad(jnp.concatenate([b_rg, b_re]).astype(F32), (0, pad)).reshape(1, LANES)
    tm = _tile(m, 256, 8)
    kern = functools.partial(_router_kernel, n_groups=n_groups, e_per_group=e_per_group)
    return pl.pallas_call(
        kern,
        grid=(m // tm,),
        in_specs=[
            pl.BlockSpec((tm, d), lambda i: (i, 0)),
            pl.BlockSpec((d, LANES), lambda i: (0, 0)),
            pl.BlockSpec((1, LANES), lambda i: (0, 0)),
        ],
        out_specs=[pl.BlockSpec((tm, LANES), lambda i: (i, 0)),
                   pl.BlockSpec((8, LANES), lambda i: (0, 0))],
        out_shape=[jax.ShapeDtypeStruct((m, LANES), F32), jax.ShapeDtypeStruct((8, LANES), F32)],
        scratch_shapes=[pltpu.VMEM((8, LANES), F32)],
        compiler_params=_params("arbitrary"),
        name="moe_router",
    )(h2d, w, bias)


def _gather_rows(idx_ref, base, n_rows, src_hbm, dst_ref, sem):
    def body(r, carry):
        row = idx_ref[base + r]
        pltpu.make_async_copy(src_hbm.at[pl.ds(row, 1)], dst_ref.at[pl.ds(r, 1)], sem).start()
        return carry

    lax.fori_loop(0, n_rows, body, 0)


def _wait_rows(n_rows, src_hbm, dst_ref, sem):
    pltpu.make_async_copy(src_hbm.at[pl.ds(0, n_rows)], dst_ref, sem).wait()


def _expert_changed(te_ref, i):
    return jnp.logical_or(i == 0, te_ref[i] != te_ref[jnp.maximum(i - 1, 0)])


def _expert_up_kernel(te_ref, nt_ref, tok_ref, h_hbm, wg_ref, wu_ref, a_ref, xbuf, wg_s, wu_s, sem,
                      *, tm, rows_per_step):
    i = pl.program_id(0)
    n_tiles = nt_ref[0]

    @pl.when(jnp.logical_and(i < n_tiles, _expert_changed(te_ref, i)))
    def _():
        _cast_rows(wg_ref, wg_s, rows_per_step)
        _cast_rows(wu_ref, wu_s, rows_per_step)

    @pl.when(i == 0)
    def _():
        _gather_rows(tok_ref, 0, tm, h_hbm, xbuf.at[0], sem.at[0])

    @pl.when(i + 1 < n_tiles)
    def _():
        nxt = (i + 1) % 2
        _gather_rows(tok_ref, (i + 1) * tm, tm, h_hbm, xbuf.at[nxt], sem.at[nxt])

    @pl.when(i < n_tiles)
    def _():
        slot = i % 2
        _wait_rows(tm, h_hbm, xbuf.at[slot], sem.at[slot])
        x = xbuf[slot].astype(BF16)
        g = jnp.dot(x, wg_s[...], preferred_element_type=F32)
        u = jnp.dot(x, wu_s[...], preferred_element_type=F32)
        a_ref[...] = (g * jax.nn.sigmoid(g) * u).astype(a_ref.dtype)

    @pl.when(i >= n_tiles)
    def _():
        a_ref[...] = jnp.zeros_like(a_ref)


def _expert_down_kernel(te_ref, nt_ref, a_ref, wd_ref, y_ref, wd_s, *, rows_per_step):
    i = pl.program_id(0)

    @pl.when(jnp.logical_and(i < nt_ref[0], _expert_changed(te_ref, i)))
    def _():
        _cast_rows(wd_ref, wd_s, rows_per_step)

    @pl.when(i < nt_ref[0])
    def _():
        y_ref[...] = jnp.dot(a_ref[...], wd_s[...], preferred_element_type=F32)

    @pl.when(i >= nt_ref[0])
    def _():
        y_ref[...] = jnp.zeros_like(y_ref)


def _experts(h2d, tile_expert, n_tiles, sorted_token, w_gate, w_up, w_down, tm, max_tiles):
    m, d = h2d.shape
    n_exp, _, f = w_gate.shape
    a = pl.pallas_call(
        functools.partial(_expert_up_kernel, tm=tm, rows_per_step=_tile(d, 256, 8)),
        grid_spec=pltpu.PrefetchScalarGridSpec(
            num_scalar_prefetch=3,
            grid=(max_tiles,),
            in_specs=[
                pl.BlockSpec(memory_space=pl.ANY),
                pl.BlockSpec((None, d, f), lambda i, te, nt, tok: (te[i], 0, 0)),
                pl.BlockSpec((None, d, f), lambda i, te, nt, tok: (te[i], 0, 0)),
            ],
            out_specs=pl.BlockSpec((tm, f), lambda i, te, nt, tok: (i, 0)),
            scratch_shapes=[pltpu.VMEM((2, tm, d), F32), pltpu.VMEM((d, f), BF16),
                            pltpu.VMEM((d, f), BF16), pltpu.SemaphoreType.DMA((2,))],
        ),
        out_shape=jax.ShapeDtypeStruct((max_tiles * tm, f), BF16),
        compiler_params=_params("arbitrary"),
        name="moe_expert_up",
    )(tile_expert, n_tiles, sorted_token, h2d, w_gate, w_up)
    return pl.pallas_call(
        functools.partial(_expert_down_kernel, rows_per_step=_tile(f, 64, 8)),
        grid_spec=pltpu.PrefetchScalarGridSpec(
            num_scalar_prefetch=2,
            grid=(max_tiles,),
            in_specs=[
                pl.BlockSpec((tm, f), lambda i, te, nt: (i, 0)),
                pl.BlockSpec((None, f, d), lambda i, te, nt: (te[i], 0, 0)),
            ],
            out_specs=pl.BlockSpec((tm, d), lambda i, te, nt: (i, 0)),
            scratch_shapes=[pltpu.VMEM((f, d), BF16)],
        ),
        out_shape=jax.ShapeDtypeStruct((max_tiles * tm, d), F32),
        compiler_params=_params("arbitrary"),
        name="moe_expert_down",
    )(tile_expert, n_tiles, a, w_down)


def _combine_kernel(p1_ref, p2_ref, y_hbm, x_ref, info_ref, gt_ref, o_ref, ybuf, sem, *, tm):
    i = pl.program_id(0)
    n_steps = pl.num_programs(0)

    def gather(step, slot):
        _gather_rows(p1_ref, step * tm, tm, y_hbm, ybuf.at[slot, 0], sem.at[slot])
        _gather_rows(p2_ref, step * tm, tm, y_hbm, ybuf.at[slot, 1], sem.at[slot])

    @pl.when(i == 0)
    def _():
        gather(0, 0)

    @pl.when(i + 1 < n_steps)
    def _():
        gather(i + 1, (i + 1) % 2)

    slot = i % 2
    _wait_rows(tm, y_hbm, ybuf.at[slot, 0], sem.at[slot])
    _wait_rows(tm, y_hbm, ybuf.at[slot, 1], sem.at[slot])
    info = info_ref[...]
    w1 = info[:, INFO_W1:INFO_W1 + 1]
    w2 = info[:, INFO_W2:INFO_W2 + 1]
    y = w1 * ybuf[slot, 0] + w2 * ybuf[slot, 1]
    o_ref[...] = x_ref[...] + gt_ref[0] * y


def _moe_combine(y_sorted, pos1, pos2, x2d, info, gate, seq):
    m, d = x2d.shape
    b = gate.shape[0]
    tm = _tile(seq, 128, 8)
    steps_per_batch = seq // tm
    return pl.pallas_call(
        functools.partial(_combine_kernel, tm=tm),
        grid_spec=pltpu.PrefetchScalarGridSpec(
            num_scalar_prefetch=2,
            grid=(m // tm,),
            in_specs=[
                pl.BlockSpec(memory_space=pl.ANY),
                pl.BlockSpec((tm, d), lambda i, p1, p2: (i, 0)),
                pl.BlockSpec((tm, LANES), lambda i, p1, p2: (i, 0)),
                pl.BlockSpec((1, 1, d), lambda i, p1, p2: (i // steps_per_batch, 0, 0)),
            ],
            out_specs=pl.BlockSpec((tm, d), lambda i, p1, p2: (i, 0)),
            scratch_shapes=[pltpu.VMEM((2, 2, tm, d), F32), pltpu.SemaphoreType.DMA((2,))],
        ),
        out_shape=jax.ShapeDtypeStruct((m, d), F32),
        compiler_params=_params("arbitrary"),
        name="moe_combine",
    )(pos1, pos2, y_sorted, x2d, info, gate.reshape(b, 1, d))


EXPERT_TILE = 256


def _hier_moe(h2d, x2d, gate, seq, layer, w_rg, b_rg, w_re, b_re, w_eg, w_eu, w_ed):
    m, d = h2d.shape
    n_layers, n_groups, e_per_group, _, f = w_eg.shape
    n_exp = n_groups * e_per_group
    tm = EXPERT_TILE
    max_tiles = (2 * m) // tm + n_exp
    info, cnt = _router(h2d, w_rg, b_rg, w_re, b_re, e_per_group)

    counts = cnt[0, :n_exp].astype(jnp.int32)
    tiles_per_expert = (counts + tm - 1) // tm
    tile_end = jnp.cumsum(tiles_per_expert)
    tile_start = tile_end - tiles_per_expert
    n_tiles = tile_end[-1:]
    expert = info[:, INFO_E1:INFO_E2 + 1].astype(jnp.int32)
    rank = info[:, INFO_R1:INFO_R2 + 1].astype(jnp.int32)
    pos = tile_start[expert] * tm + rank
    tile_ids = jnp.arange(max_tiles, dtype=jnp.int32)
    tile_expert = jnp.sum((tile_ids[:, None] >= tile_end[None, :]).astype(jnp.int32), axis=1)
    last_expert = tile_expert[jnp.maximum(n_tiles[0] - 1, 0)]
    tile_expert = jnp.where(tile_ids < n_tiles[0], tile_expert, last_expert) + layer * n_exp
    token = jnp.broadcast_to(jnp.arange(m, dtype=jnp.int32)[:, None], (m, 2))
    sorted_token = jnp.zeros((max_tiles * tm,), jnp.int32).at[pos.reshape(-1)].set(token.reshape(-1))

    y_sorted = _experts(h2d, tile_expert, n_tiles, sorted_token,
                        w_eg.reshape(n_layers * n_exp, d, f), w_eu.reshape(n_layers * n_exp, d, f),
                        w_ed.reshape(n_layers * n_exp, f, d), tm, max_tiles)
    return _moe_combine(y_sorted, pos[:, 0], pos[:, 1], x2d, info, gate, seq)


def kernel(x, c, ada_w, ada_b, norm_mix_g, norm_ffn_g, w_in, b_forget, rel_bias, w_branch_a, w_branch_b, w_gate, b_gate, w_out, w_router_group, b_router_group, w_router_expert, b_router_expert, w_exp_gate, w_exp_up, w_exp_down, final_norm_g):
    b, s, d = x.shape
    n_layers = ada_w.shape[0]
    m = b * s
    heads_a, heads_b = b_forget.shape[1], rel_bias.shape[1]
    wa3 = 3 * heads_a * HEAD_DIM
    assert w_branch_a.shape[1] == heads_a * HEAD_DIM and w_branch_b.shape[1] == heads_b * HEAD_DIM
    assert w_in.shape[2] == wa3 + heads_a + 3 * heads_b * HEAD_DIM

    mod = _adaln_mod(c, ada_w, ada_b)
    w_in_t = jnp.swapaxes(w_in, 1, 2)
    for l in range(n_layers):
        sh_m, sc_m, gt_m, sh_f, sc_f, gt_f = [mod[l, :, i * d:(i + 1) * d] for i in range(6)]
        h = _norm_mod(x, norm_mix_g[l], sh_m, sc_m, BF16)
        h2d = h.reshape(m, d)
        qkv = _in_projection(h2d, w_in_t, l, wa3, heads_a).reshape(b, s, -1)
        cum = _forget_cumsum(h, w_in_t, l, b_forget[l], wa3)
        ya = _fox_attention(qkv, cum, heads_a, (0, heads_a, 2 * heads_a))
        off = 3 * heads_a
        yb = _chunk_attention(qkv, rel_bias[l], (off, off + heads_b, off + 2 * heads_b))
        merged = _gated_merge(h2d, ya.reshape(m, -1), yb.reshape(m, -1), l, w_gate, b_gate[l],
                              w_branch_a, w_branch_b)
        x2d = _out_projection(merged, w_out, l, x.reshape(m, d), gt_m, s)
        h_ffn = _norm_mod(x2d.reshape(b, s, d), norm_ffn_g[l], sh_f, sc_f, F32).reshape(m, d)
        x2d = _hier_moe(h_ffn, x2d, gt_f, s, l, w_router_group[l], b_router_group[l],
                        w_router_expert[l], b_router_expert[l],
                        w_exp_gate, w_exp_up, w_exp_down)
        x = x2d.reshape(b, s, d)
    zeros = jnp.zeros((b, d), F32)
    return _norm_mod(x, final_norm_g, zeros, zeros, x.dtype)
```

```python
import functools

import jax
import jax.numpy as jnp
from jax import lax
from jax.experimental import pallas as pl
from jax.experimental.pallas import tpu as pltpu

CHUNK = 64
LEFT_CHUNKS = 8
EPS = 1e-6
NEG_INF = -1e30
LANES = 128
HEAD_DIM = 128
LOG2E = 1.4426950408889634
V7X_VMEM_LIMIT_BYTES = 60000 * 1024

F32 = jnp.float32
BF16 = jnp.bfloat16


def _params(*sem):
    return pltpu.CompilerParams(dimension_semantics=sem, vmem_limit_bytes=V7X_VMEM_LIMIT_BYTES)


def _tile(n, pref, mult=LANES):
    t = (min(pref, n) // mult) * mult
    while t >= mult:
        if n % t == 0:
            return t
        t -= mult
    return n


def _cast_rows(src_ref, dst_ref, rows_per_step):
    n = src_ref.shape[0]

    def body(r, carry):
        rows = pl.ds(pl.multiple_of(r * rows_per_step, rows_per_step), rows_per_step)
        dst_ref[rows, :] = src_ref[rows, :].astype(dst_ref.dtype)
        return carry

    lax.fori_loop(0, n // rows_per_step, body, 0)


def _mod_kernel(c_ref, w_ref, b_ref, o_ref):
    c = c_ref[...]
    c_act = (c * jax.nn.sigmoid(c)).astype(BF16)
    o_ref[0] = jnp.dot(c_act, w_ref[0].astype(BF16), preferred_element_type=F32) + b_ref[0]


def _adaln_mod(c, ada_w, ada_b):
    n_layers, d, n = ada_w.shape
    b = c.shape[0]
    bp = -(-b // 8) * 8
    c_pad = jnp.pad(c, ((0, bp - b), (0, 0)))
    tn = _tile(n, 512)
    out = pl.pallas_call(
        _mod_kernel,
        grid=(n_layers, n // tn),
        in_specs=[
            pl.BlockSpec((bp, d), lambda l, j: (0, 0)),
            pl.BlockSpec((1, d, tn), lambda l, j: (l, 0, j)),
            pl.BlockSpec((1, 1, tn), lambda l, j: (l, 0, j)),
        ],
        out_specs=pl.BlockSpec((1, bp, tn), lambda l, j: (l, 0, j)),
        out_shape=jax.ShapeDtypeStruct((n_layers, bp, n), F32),
        compiler_params=_params("arbitrary", "arbitrary"),
        name="adaln_mod",
    )(c_pad, ada_w, ada_b.reshape(n_layers, 1, n))
    return out[:, :b]


def _norm_mod_kernel(x_ref, g_ref, sh_ref, sc_ref, o_ref):
    x = x_ref[0]
    ms = jnp.mean(x * x, axis=-1, keepdims=True)
    y = x * lax.rsqrt(ms + EPS) * g_ref[...]
    o_ref[0] = (y * (1.0 + sc_ref[0]) + sh_ref[0]).astype(o_ref.dtype)


def _norm_mod(x, g, shift, scale, out_dtype):
    b, s, d = x.shape
    ts = _tile(s, 256, 8)
    return pl.pallas_call(
        _norm_mod_kernel,
        grid=(b, s // ts),
        in_specs=[
            pl.BlockSpec((1, ts, d), lambda i, j: (i, j, 0)),
            pl.BlockSpec((1, d), lambda i, j: (0, 0)),
            pl.BlockSpec((1, 1, d), lambda i, j: (i, 0, 0)),
            pl.BlockSpec((1, 1, d), lambda i, j: (i, 0, 0)),
        ],
        out_specs=pl.BlockSpec((1, ts, d), lambda i, j: (i, j, 0)),
        out_shape=jax.ShapeDtypeStruct((b, s, d), out_dtype),
        compiler_params=_params("arbitrary", "arbitrary"),
        name="norm_mod",
    )(x, g.reshape(1, d), shift.reshape(b, 1, d), scale.reshape(b, 1, d))


def _dot_nt(a, b):
    return lax.dot_general(a, b, (((1,), (1,)), ((), ())), preferred_element_type=F32)


def _inproj_kernel(h_ref, w_ref, o_ref, wbf_ref, *, rows_per_step):
    @pl.when(pl.program_id(1) == 0)
    def _():
        _cast_rows(w_ref, wbf_ref, rows_per_step)

    o_ref[...] = _dot_nt(h_ref[...], wbf_ref[...]).astype(o_ref.dtype)


def _in_projection(h2d, w_in_t, layer, w_a3, n_forget):
    m, d = h2d.shape
    n_out = w_in_t.shape[1] - n_forget
    tn = _tile(w_a3, 512)
    assert n_out % tn == 0 and w_a3 % tn == 0 and n_forget % 8 == 0
    tm = _tile(m, 1024, 8)
    n_plain = w_a3 // tn
    kern = functools.partial(_inproj_kernel, rows_per_step=_tile(tn, 128, 8))
    return pl.pallas_call(
        kern,
        grid=(n_out // tn, m // tm),
        in_specs=[
            pl.BlockSpec((tm, d), lambda j, i: (i, 0)),
            pl.BlockSpec((None, pl.Element(tn), pl.Element(d)),
                         lambda j, i: (layer, pl.multiple_of(
                             j * tn + jnp.where(j >= n_plain, n_forget, 0), 8), 0)),
        ],
        out_specs=pl.BlockSpec((tm, tn), lambda j, i: (i, j)),
        out_shape=jax.ShapeDtypeStruct((m, n_out), BF16),
        scratch_shapes=[pltpu.VMEM((tn, d), BF16)],
        compiler_params=_params("arbitrary", "arbitrary"),
        name="in_projection",
    )(h2d, w_in_t)


def _split3_dot(x, u):
    x1 = x.astype(BF16)
    r1 = x - x1.astype(F32)
    x2 = r1.astype(BF16)
    x3 = (r1 - x2.astype(F32)).astype(BF16)
    ub = u.astype(BF16)
    return (jnp.dot(x1, ub, preferred_element_type=F32)
            + jnp.dot(x2, ub, preferred_element_type=F32)
            + jnp.dot(x3, ub, preferred_element_type=F32))


def _forget_kernel(h_ref, w_ref, b_ref, o_ref, carry_ref):
    s = pl.program_id(1)

    @pl.when(s == 0)
    def _():
        carry_ref[...] = jnp.zeros_like(carry_ref)

    ts = h_ref.shape[1]
    f = _dot_nt(w_ref[...].astype(BF16), h_ref[0]) + b_ref[...]
    log_f = jnp.minimum(f, 0.0) - jnp.log1p(jnp.exp(-jnp.abs(f)))
    row = lax.broadcasted_iota(jnp.int32, (ts, ts), 0)
    col = lax.broadcasted_iota(jnp.int32, (ts, ts), 1)
    upper = jnp.where(row <= col, 1.0, 0.0).astype(F32)
    cs = _split3_dot(log_f, upper) + carry_ref[:, 0:1]
    o_ref[0] = cs * LOG2E
    carry_ref[...] = jnp.broadcast_to(cs[:, ts - 1:ts], carry_ref.shape)


def _forget_cumsum(h, w_in_t, layer, b_forget, row0):
    b, s, d = h.shape
    n_heads = b_forget.shape[0]
    hp = -(-n_heads // 8) * 8
    assert row0 % 8 == 0
    ts = _tile(s, 512)
    bias = jnp.pad(b_forget.astype(F32), (0, hp - n_heads)).reshape(hp, 1)
    return pl.pallas_call(
        _forget_kernel,
        grid=(b, s // ts),
        in_specs=[
            pl.BlockSpec((1, ts, d), lambda i, j: (i, j, 0)),
            pl.BlockSpec((None, pl.Element(hp), pl.Element(d)), lambda i, j: (layer, row0, 0)),
            pl.BlockSpec((hp, 1), lambda i, j: (0, 0)),
        ],
        out_specs=pl.BlockSpec((1, hp, ts), lambda i, j: (i, 0, j)),
        out_shape=jax.ShapeDtypeStruct((b, hp, s), F32),
        scratch_shapes=[pltpu.VMEM((hp, LANES), F32)],
        compiler_params=_params("arbitrary", "arbitrary"),
        name="forget_cumsum",
    )(h, w_in_t, bias)


def _fox_kernel(q_ref, k_ref, v_ref, cum_ref, o_ref, *, tq, scale_log2e):
    head = pl.program_id(1)
    seq = q_ref.shape[1]
    row = lax.broadcasted_iota(jnp.int32, (tq, tq), 0)
    col = lax.broadcasted_iota(jnp.int32, (tq, tq), 1)
    causal = row >= col
    decay = cum_ref[0, pl.ds(head, 1), :]

    n_blocks = seq // tq

    def scores(iq):
        q0 = iq * tq
        q = q_ref[0, q0:q0 + tq, :]
        s_diag = _dot_nt(q, k_ref[0, q0:q0 + tq, :]) * scale_log2e - decay[:, q0:q0 + tq]
        s_diag = jnp.where(causal, s_diag, NEG_INF)
        s_past = None
        if iq > 0:
            s_past = _dot_nt(q, k_ref[0, 0:q0, :]) * scale_log2e - decay[:, 0:q0]
        return s_diag, s_past

    nxt = scores(0)
    for iq in range(n_blocks):
        q0 = iq * tq
        s_diag, s_past = nxt
        if iq + 1 < n_blocks:
            nxt = scores(iq + 1)
        m = jnp.max(s_diag, axis=-1, keepdims=True)
        if iq > 0:
            m = jnp.maximum(m, jnp.max(s_past, axis=-1, keepdims=True))
        p_diag = jnp.exp2(s_diag - m)
        l = jnp.sum(p_diag, axis=-1, keepdims=True)
        acc = jnp.dot(p_diag.astype(BF16), v_ref[0, q0:q0 + tq, :], preferred_element_type=F32)
        if iq > 0:
            p_past = jnp.exp2(s_past - m)
            l = l + jnp.sum(p_past, axis=-1, keepdims=True)
            acc = acc + jnp.dot(p_past.astype(BF16), v_ref[0, 0:q0, :], preferred_element_type=F32)
        o_ref[0, q0:q0 + tq, :] = (acc / l).astype(o_ref.dtype)


def _fox_attention(qkv, cum, n_heads, col_blocks):
    b, s, _ = qkv.shape
    hp = cum.shape[1]
    tq = _tile(s, 256)
    qc, kc, vc = col_blocks
    kern = functools.partial(_fox_kernel, tq=tq, scale_log2e=HEAD_DIM ** -0.5 * LOG2E)
    return pl.pallas_call(
        kern,
        grid=(b, n_heads),
        in_specs=[
            pl.BlockSpec((1, s, HEAD_DIM), lambda i, h: (i, 0, qc + h)),
            pl.BlockSpec((1, s, HEAD_DIM), lambda i, h: (i, 0, kc + h)),
            pl.BlockSpec((1, s, HEAD_DIM), lambda i, h: (i, 0, vc + h)),
            pl.BlockSpec((1, hp, s), lambda i, h: (i, 0, 0)),
        ],
        out_specs=pl.BlockSpec((1, s, HEAD_DIM), lambda i, h: (i, 0, h)),
        out_shape=jax.ShapeDtypeStruct((b, s, n_heads * HEAD_DIM), BF16),
        compiler_params=_params("arbitrary", "arbitrary"),
        name="fox_attention",
    )(qkv, qkv, qkv, cum)


Q_CHUNKS = 2
QB = Q_CHUNKS * CHUNK
PAD = LEFT_CHUNKS * CHUNK
WIN = (LEFT_CHUNKS + Q_CHUNKS) * CHUNK
CHUNK_SHIFT = CHUNK.bit_length() - 1


def _chunk_bias_kernel(tab_ref, o_ref, *, max_rel, c0):
    h = pl.program_id(0)

    def visible(qi, kj):
        q_chunk = jnp.right_shift(qi, CHUNK_SHIFT)
        k_chunk = jnp.right_shift(kj, CHUNK_SHIFT)
        return jnp.logical_and(k_chunk >= q_chunk, k_chunk <= q_chunk + LEFT_CHUNKS)

    if c0 > 0:
        qi = lax.broadcasted_iota(jnp.int32, (QB, c0), 0)
        kj = lax.broadcasted_iota(jnp.int32, (QB, c0), 1)
        far = tab_ref[h, 2 * max_rel] * LOG2E
        o_ref[0, :, 0:c0] = jnp.where(visible(qi, kj), far, NEG_INF)
    qi = lax.broadcasted_iota(jnp.int32, (QB, WIN - c0), 0)
    kj = lax.broadcasted_iota(jnp.int32, (QB, WIN - c0), 1) + c0
    idx = jnp.clip(PAD + qi - kj, -max_rel, max_rel) + max_rel

    def body(r, acc):
        return jnp.where(idx == r, tab_ref[h, r], acc)

    near = lax.fori_loop(0, 2 * max_rel + 1, body, jnp.zeros((QB, WIN - c0), F32), unroll=8)
    o_ref[0, :, c0:WIN] = jnp.where(visible(qi, kj), near * LOG2E, NEG_INF)


def _chunk_bias(rel_table):
    n_heads, n_rel = rel_table.shape
    max_rel = (n_rel - 1) // 2
    c0 = max(0, (PAD - max_rel + 1) // LANES * LANES)
    return pl.pallas_call(
        functools.partial(_chunk_bias_kernel, max_rel=max_rel, c0=c0),
        grid=(n_heads,),
        in_specs=[pl.BlockSpec(memory_space=pltpu.SMEM)],
        out_specs=pl.BlockSpec((1, QB, WIN), lambda h: (h, 0, 0)),
        out_shape=jax.ShapeDtypeStruct((n_heads, QB, WIN), F32),
        compiler_params=_params("arbitrary"),
        name="chunk_bias",
    )(rel_table.astype(F32))


def _chunk_kernel(q_ref, k_ref, v_ref, bias_ref, o_ref, *, scale_log2e):
    seq = q_ref.shape[1]
    n_blocks = seq // QB

    def scores(i):
        q0 = i * QB
        k_lo = max(0, q0 - PAD)
        width = q0 + QB - k_lo
        q = q_ref[0, q0:q0 + QB, :]
        return _dot_nt(q, k_ref[0, k_lo:q0 + QB, :]) * scale_log2e + bias_ref[0, :, WIN - width:WIN]

    s_next = scores(0)
    for i in range(n_blocks):
        q0 = i * QB
        k_lo = max(0, q0 - PAD)
        s = s_next
        if i + 1 < n_blocks:
            s_next = scores(i + 1)
        m = jnp.max(s, axis=-1, keepdims=True)
        p = jnp.exp2(s - m)
        l = jnp.sum(p, axis=-1, keepdims=True)
        out = jnp.dot(p.astype(BF16), v_ref[0, k_lo:q0 + QB, :], preferred_element_type=F32)
        o_ref[0, q0:q0 + QB, :] = (out / l).astype(o_ref.dtype)


def _chunk_attention(qkv, rel_table, col_blocks):
    b, s, _ = qkv.shape
    n_heads = rel_table.shape[0]
    assert s % QB == 0
    qc, kc, vc = col_blocks
    kern = functools.partial(_chunk_kernel, scale_log2e=HEAD_DIM ** -0.5 * LOG2E)
    return pl.pallas_call(
        kern,
        grid=(b, n_heads),
        in_specs=[
            pl.BlockSpec((1, s, HEAD_DIM), lambda i, h: (i, 0, qc + h)),
            pl.BlockSpec((1, s, HEAD_DIM), lambda i, h: (i, 0, kc + h)),
            pl.BlockSpec((1, s, HEAD_DIM), lambda i, h: (i, 0, vc + h)),
            pl.BlockSpec((1, QB, WIN), lambda i, h: (h, 0, 0)),
        ],
        out_specs=pl.BlockSpec((1, s, HEAD_DIM), lambda i, h: (i, 0, h)),
        out_shape=jax.ShapeDtypeStruct((b, s, n_heads * HEAD_DIM), BF16),
        compiler_params=_params("arbitrary", "arbitrary"),
        name="chunk_attention",
    )(qkv, qkv, qkv, _chunk_bias(rel_table))


def _merge_kernel(h_ref, ya_ref, yb_ref, wga_ref, wgb_ref, wa_ref, wb_ref, bga_ref, bgb_ref, o_ref,
                  wga_s, wgb_s, wa_s, wb_s, *, rows_per_step):
    @pl.when(pl.program_id(1) == 0)
    def _():
        _cast_rows(wga_ref, wga_s, rows_per_step)
        _cast_rows(wgb_ref, wgb_s, rows_per_step)
        _cast_rows(wa_ref, wa_s, rows_per_step)
        _cast_rows(wb_ref, wb_s, rows_per_step)

    h = h_ref[...]
    gate_a = jax.nn.sigmoid(jnp.dot(h, wga_s[...], preferred_element_type=F32) + bga_ref[...])
    gate_b = jax.nn.sigmoid(jnp.dot(h, wgb_s[...], preferred_element_type=F32) + bgb_ref[...])
    proj_a = jnp.dot(ya_ref[...], wa_s[...], preferred_element_type=F32)
    proj_b = jnp.dot(yb_ref[...], wb_s[...], preferred_element_type=F32)
    o_ref[...] = (gate_a * proj_a + gate_b * proj_b).astype(o_ref.dtype)


def _gated_merge(h2d, ya2d, yb2d, layer, w_gate, b_gate, w_a, w_b):
    m, d = h2d.shape
    wa_rows, wb_rows = w_a.shape[1], w_b.shape[1]
    tn = _tile(d, 256)
    tm = _tile(m, 1024, 8)
    nb = d // tn
    single = pl.Buffered(1)
    kern = functools.partial(_merge_kernel, rows_per_step=_tile(min(d, wa_rows, wb_rows), 256, 8))
    b2 = b_gate.reshape(1, 2 * d)
    return pl.pallas_call(
        kern,
        grid=(nb, m // tm),
        in_specs=[
            pl.BlockSpec((tm, d), lambda j, i: (i, 0)),
            pl.BlockSpec((tm, wa_rows), lambda j, i: (i, 0)),
            pl.BlockSpec((tm, wb_rows), lambda j, i: (i, 0)),
            pl.BlockSpec((None, d, tn), lambda j, i: (layer, 0, j), pipeline_mode=single),
            pl.BlockSpec((None, d, tn), lambda j, i: (layer, 0, nb + j), pipeline_mode=single),
            pl.BlockSpec((None, wa_rows, tn), lambda j, i: (layer, 0, j), pipeline_mode=single),
            pl.BlockSpec((None, wb_rows, tn), lambda j, i: (layer, 0, j), pipeline_mode=single),
            pl.BlockSpec((1, tn), lambda j, i: (0, j)),
            pl.BlockSpec((1, tn), lambda j, i: (0, nb + j)),
        ],
        out_specs=pl.BlockSpec((tm, tn), lambda j, i: (i, j)),
        out_shape=jax.ShapeDtypeStruct((m, d), BF16),
        scratch_shapes=[pltpu.VMEM((d, tn), BF16), pltpu.VMEM((d, tn), BF16),
                        pltpu.VMEM((wa_rows, tn), BF16), pltpu.VMEM((wb_rows, tn), BF16)],
        compiler_params=_params("arbitrary", "arbitrary"),
        name="gated_merge",
    )(h2d, ya2d, yb2d, w_gate, w_gate, w_a, w_b, b2, b2)


def _outproj_kernel(a_ref, w_ref, x_ref, gt_ref, o_ref, w_s, *, rows_per_step):
    @pl.when(pl.program_id(1) == 0)
    def _():
        _cast_rows(w_ref, w_s, rows_per_step)

    y = jnp.dot(a_ref[...], w_s[...], preferred_element_type=F32)
    o_ref[...] = x_ref[...] + gt_ref[0] * y


def _out_projection(a2d, w_out, layer, x2d, gate, seq):
    m, k = a2d.shape
    d = w_out.shape[2]
    b = gate.shape[0]
    tn = _tile(d, 512)
    tm = _tile(seq, 1024, 8)
    steps_per_batch = seq // tm
    kern = functools.partial(_outproj_kernel, rows_per_step=_tile(k, 256, 8))
    return pl.pallas_call(
        kern,
        grid=(d // tn, m // tm),
        in_specs=[
            pl.BlockSpec((tm, k), lambda j, i: (i, 0)),
            pl.BlockSpec((None, k, tn), lambda j, i: (layer, 0, j), pipeline_mode=pl.Buffered(1)),
            pl.BlockSpec((tm, tn), lambda j, i: (i, j)),
            pl.BlockSpec((1, 1, tn), lambda j, i: (i // steps_per_batch, 0, j)),
        ],
        out_specs=pl.BlockSpec((tm, tn), lambda j, i: (i, j)),
        out_shape=jax.ShapeDtypeStruct((m, d), F32),
        scratch_shapes=[pltpu.VMEM((k, tn), BF16)],
        compiler_params=_params("arbitrary", "arbitrary"),
        name="out_projection",
    )(a2d, w_out, x2d, gate.reshape(b, 1, d))


INFO_E1, INFO_E2, INFO_W1, INFO_W2, INFO_R1, INFO_R2 = range(6)


def _router_kernel(x_ref, g_ref, sh_ref, sc_ref, w_ref, b_ref, h_ref, info_ref, cnt_ref, carry_ref,
                   *, n_groups, e_per_group):
    @pl.when(pl.program_id(0) == 0)
    def _():
        carry_ref[...] = jnp.zeros_like(carry_ref)

    tm = x_ref.shape[0]
    x = x_ref[...]
    ms = jnp.mean(x * x, axis=-1, keepdims=True)
    h = x * lax.rsqrt(ms + EPS) * g_ref[...] * (1.0 + sc_ref[0]) + sh_ref[0]
    h_ref[...] = h
    logits = jnp.dot(h.astype(BF16), w_ref[...].astype(BF16),
                     preferred_element_type=F32) + b_ref[...]
    lane = lax.broadcasted_iota(jnp.int32, (tm, LANES), 1).astype(F32)
    big = float(LANES)
    is_group = lane < n_groups
    gl = jnp.where(is_group, logits, -jnp.inf)
    g_max = jnp.max(gl, axis=-1, keepdims=True)
    g_sel = jnp.min(jnp.where(gl == g_max, lane, big), axis=-1, keepdims=True)
    g_w = 1.0 / jnp.sum(jnp.where(is_group, jnp.exp(gl - g_max), 0.0), axis=-1, keepdims=True)

    expert = lane - n_groups
    lo = g_sel * e_per_group
    in_group = jnp.logical_and(expert >= lo, expert < lo + e_per_group)
    el = jnp.where(in_group, logits, -jnp.inf)
    v1 = jnp.max(el, axis=-1, keepdims=True)
    e1 = jnp.min(jnp.where(el == v1, expert, big), axis=-1, keepdims=True)
    el2 = jnp.where(expert == e1, -jnp.inf, el)
    v2 = jnp.max(el2, axis=-1, keepdims=True)
    e2 = jnp.min(jnp.where(el2 == v2, expert, big), axis=-1, keepdims=True)
    t = jnp.exp(v2 - v1)
    w1 = g_w / (1.0 + t)
    w2 = g_w * t / (1.0 + t)

    onehot = jnp.where(jnp.logical_or(lane == e1, lane == e2), 1.0, 0.0)
    row = lax.broadcasted_iota(jnp.int32, (tm, tm), 0)
    col = lax.broadcasted_iota(jnp.int32, (tm, tm), 1)
    before = jnp.where(col < row, 1.0, 0.0).astype(BF16)
    prior = jnp.dot(before, onehot.astype(BF16), preferred_element_type=F32) + carry_ref[0:1, :]
    r1 = jnp.sum(jnp.where(lane == e1, prior, 0.0), axis=-1, keepdims=True)
    r2 = jnp.sum(jnp.where(lane == e2, prior, 0.0), axis=-1, keepdims=True)
    total = carry_ref[0:1, :] + jnp.sum(onehot, axis=0, keepdims=True)
    carry_ref[...] = jnp.broadcast_to(total, carry_ref.shape)
    cnt_ref[...] = jnp.broadcast_to(total, cnt_ref.shape)

    info = jnp.zeros((tm, LANES), F32)
    for idx, val in ((INFO_E1, e1), (INFO_E2, e2), (INFO_W1, w1), (INFO_W2, w2),
                     (INFO_R1, r1), (INFO_R2, r2)):
        info = jnp.where(lane == idx, val, info)
    info_ref[...] = info


def _norm_router(x2d, g, shift, scale, seq, w_rg, b_rg, w_re, b_re, e_per_group):
    m, d = x2d.shape
    b = shift.shape[0]
    n_groups = w_rg.shape[1]
    n_exp = w_re.shape[1]
    assert n_groups + n_exp <= LANES
    pad = LANES - n_groups - n_exp
    w = jnp.pad(jnp.concatenate([w_rg, w_re], axis=1), ((0, 0), (0, pad)))
    bias = jnp.pad(jnp.concatenate([b_rg, b_re]).astype(F32), (0, pad)).reshape(1, LANES)
    tm = _tile(seq, 256, 8)
    steps_per_batch = seq // tm
    kern = functools.partial(_router_kernel, n_groups=n_groups, e_per_group=e_per_group)
    return pl.pallas_call(
        kern,
        grid=(m // tm,),
        in_specs=[
            pl.BlockSpec((tm, d), lambda i: (i, 0)),
            pl.BlockSpec((1, d), lambda i: (0, 0)),
            pl.BlockSpec((1, 1, d), lambda i: (i // steps_per_batch, 0, 0)),
            pl.BlockSpec((1, 1, d), lambda i: (i // steps_per_batch, 0, 0)),
            pl.BlockSpec((d, LANES), lambda i: (0, 0)),
            pl.BlockSpec((1, LANES), lambda i: (0, 0)),
        ],
        out_specs=[pl.BlockSpec((tm, d), lambda i: (i, 0)),
                   pl.BlockSpec((tm, LANES), lambda i: (i, 0)),
                   pl.BlockSpec((8, LANES), lambda i: (0, 0))],
        out_shape=[jax.ShapeDtypeStruct((m, d), F32), jax.ShapeDtypeStruct((m, LANES), F32),
                   jax.ShapeDtypeStruct((8, LANES), F32)],
        scratch_shapes=[pltpu.VMEM((8, LANES), F32)],
        compiler_params=_params("arbitrary"),
        name="moe_norm_router",
    )(x2d, g.reshape(1, d), shift.reshape(b, 1, d), scale.reshape(b, 1, d), w, bias)


def _gather_rows(idx_ref, base, n_rows, src_hbm, dst_ref, sem):
    def body(r, carry):
        row = idx_ref[base + r]
        pltpu.make_async_copy(src_hbm.at[pl.ds(row, 1)], dst_ref.at[pl.ds(r, 1)], sem).start()
        return carry

    lax.fori_loop(0, n_rows, body, 0, unroll=8)


def _wait_rows(n_rows, src_hbm, dst_ref, sem):
    pltpu.make_async_copy(src_hbm.at[pl.ds(0, n_rows)], dst_ref, sem).wait()


def _expert_changed(te_ref, i):
    return jnp.logical_or(i == 0, te_ref[i] != te_ref[jnp.maximum(i - 1, 0)])


def _expert_up_kernel(te_ref, nt_ref, tok_ref, h_hbm, wg_ref, wu_ref, a_ref, xbuf, wg_s, wu_s, sem,
                      *, tm, rows_per_step, k_chunks):
    i = pl.program_id(0)
    n_tiles = nt_ref[0]
    d = wg_ref.shape[0]
    kc = d // k_chunks
    rows_per_chunk = tm // k_chunks

    @pl.when(jnp.logical_and(i < n_tiles, _expert_changed(te_ref, i)))
    def _():
        _cast_rows(wg_ref, wg_s, rows_per_step)
        _cast_rows(wu_ref, wu_s, rows_per_step)

    @pl.when(i == 0)
    def _():
        _gather_rows(tok_ref, 0, tm, h_hbm, xbuf.at[0], sem.at[0])

    @pl.when(i < n_tiles)
    def _():
        slot = i % 2
        nxt = 1 - slot
        _wait_rows(tm, h_hbm, xbuf.at[slot], sem.at[slot])
        g = u = None
        for c in range(k_chunks):
            for r in range(c * rows_per_chunk, (c + 1) * rows_per_chunk):
                row = tok_ref[(i + 1) * tm + r]
                pltpu.make_async_copy(h_hbm.at[pl.ds(row, 1)], xbuf.at[nxt, pl.ds(r, 1)],
                                      sem.at[nxt]).start()
            x = xbuf[slot, :, c * kc:(c + 1) * kc].astype(BF16)
            gc = jnp.dot(x, wg_s[c * kc:(c + 1) * kc, :], preferred_element_type=F32)
            uc = jnp.dot(x, wu_s[c * kc:(c + 1) * kc, :], preferred_element_type=F32)
            g = gc if g is None else g + gc
            u = uc if u is None else u + uc
        a_ref[...] = (g * jax.nn.sigmoid(g) * u).astype(a_ref.dtype)

    @pl.when(i == n_tiles)
    def _():
        _wait_rows(tm, h_hbm, xbuf.at[i % 2], sem.at[i % 2])

    @pl.when(i >= n_tiles)
    def _():
        a_ref[...] = jnp.zeros_like(a_ref)


def _expert_down_kernel(te_ref, nt_ref, a_ref, wd_ref, y_ref, wd_s, *, rows_per_step):
    i = pl.program_id(0)

    @pl.when(jnp.logical_and(i < nt_ref[0], _expert_changed(te_ref, i)))
    def _():
        _cast_rows(wd_ref, wd_s, rows_per_step)

    @pl.when(i < nt_ref[0])
    def _():
        y_ref[...] = jnp.dot(a_ref[...], wd_s[...], preferred_element_type=F32)

    @pl.when(i >= nt_ref[0])
    def _():
        y_ref[...] = jnp.zeros_like(y_ref)


def _experts(h2d, tile_expert, n_tiles, sorted_token, w_gate, w_up, w_down, tm, max_tiles):
    m, d = h2d.shape
    n_exp, _, f = w_gate.shape
    k_chunks = min(8, d // LANES)
    assert d % (k_chunks * LANES) == 0 and tm % k_chunks == 0
    a = pl.pallas_call(
        functools.partial(_expert_up_kernel, tm=tm, rows_per_step=_tile(d, 256, 8),
                          k_chunks=k_chunks),
        grid_spec=pltpu.PrefetchScalarGridSpec(
            num_scalar_prefetch=3,
            grid=(max_tiles + 1,),
            in_specs=[
                pl.BlockSpec(memory_space=pl.ANY),
                pl.BlockSpec((None, d, f), lambda i, te, nt, tok: (te[i], 0, 0)),
                pl.BlockSpec((None, d, f), lambda i, te, nt, tok: (te[i], 0, 0)),
            ],
            out_specs=pl.BlockSpec((tm, f), lambda i, te, nt, tok: (i, 0)),
            scratch_shapes=[pltpu.VMEM((2, tm, d), F32), pltpu.VMEM((d, f), BF16),
                            pltpu.VMEM((d, f), BF16), pltpu.SemaphoreType.DMA((2,))],
        ),
        out_shape=jax.ShapeDtypeStruct(((max_tiles + 1) * tm, f), BF16),
        compiler_params=_params("arbitrary"),
        name="moe_expert_up",
    )(tile_expert, n_tiles, sorted_token, h2d, w_gate, w_up)
    return pl.pallas_call(
        functools.partial(_expert_down_kernel, rows_per_step=_tile(f, 64, 8)),
        grid_spec=pltpu.PrefetchScalarGridSpec(
            num_scalar_prefetch=2,
            grid=(max_tiles,),
            in_specs=[
                pl.BlockSpec((tm, f), lambda i, te, nt: (i, 0)),
                pl.BlockSpec((None, f, d), lambda i, te, nt: (te[i], 0, 0)),
            ],
            out_specs=pl.BlockSpec((tm, d), lambda i, te, nt: (i, 0)),
            scratch_shapes=[pltpu.VMEM((f, d), BF16)],
        ),
        out_shape=jax.ShapeDtypeStruct((max_tiles * tm, d), F32),
        compiler_params=_params("arbitrary"),
        name="moe_expert_down",
    )(tile_expert, n_tiles, a, w_down)


def _combine_kernel(p1_ref, p2_ref, y_hbm, x_ref, info_ref, gt_ref, g_ref, sh_ref, sc_ref, *refs,
                    tm, emit_x, col_chunks):
    if emit_x:
        x_out, h_out, ybuf, xs, sem = refs
    else:
        h_out, ybuf, xs, sem = refs
    i = pl.program_id(0)
    n_steps = pl.num_programs(0)
    d = x_ref.shape[1]
    cw = d // col_chunks
    rows_per_chunk = tm // col_chunks

    @pl.when(i == 0)
    def _():
        _gather_rows(p1_ref, 0, tm, y_hbm, ybuf.at[0, 0], sem.at[0])
        _gather_rows(p2_ref, 0, tm, y_hbm, ybuf.at[0, 1], sem.at[0])

    slot = i % 2
    nxt = 1 - slot
    nxt_base = jnp.minimum(i + 1, n_steps - 1) * tm
    _wait_rows(tm, y_hbm, ybuf.at[slot, 0], sem.at[slot])
    _wait_rows(tm, y_hbm, ybuf.at[slot, 1], sem.at[slot])
    info = info_ref[...]
    w1 = info[:, INFO_W1:INFO_W1 + 1]
    w2 = info[:, INFO_W2:INFO_W2 + 1]
    ssq = jnp.zeros((tm, 1), F32)
    for c in range(col_chunks):
        for r in range(c * rows_per_chunk, (c + 1) * rows_per_chunk):
            for k, p_ref in enumerate((p1_ref, p2_ref)):
                pltpu.make_async_copy(y_hbm.at[pl.ds(p_ref[nxt_base + r], 1)],
                                      ybuf.at[nxt, k, pl.ds(r, 1)], sem.at[nxt]).start()
        cols = slice(c * cw, (c + 1) * cw)
        y = w1 * ybuf[slot, 0, :, cols] + w2 * ybuf[slot, 1, :, cols]
        xc = x_ref[:, cols] + gt_ref[0][:, cols] * y
        xs[:, cols] = xc
        ssq = ssq + jnp.sum(xc * xc, axis=-1, keepdims=True)
    x = xs[...]
    if emit_x:
        x_out[...] = x
    h = x * lax.rsqrt(ssq * (1.0 / d) + EPS) * g_ref[...] * (1.0 + sc_ref[0]) + sh_ref[0]
    h_out[...] = h.astype(h_out.dtype)

    @pl.when(i == n_steps - 1)
    def _():
        _wait_rows(tm, y_hbm, ybuf.at[nxt, 0], sem.at[nxt])
        _wait_rows(tm, y_hbm, ybuf.at[nxt, 1], sem.at[nxt])


def _moe_combine(y_sorted, pos1, pos2, x2d, info, gate, seq, norm_g, norm_shift, norm_scale,
                 norm_dtype, emit_x):
    m, d = x2d.shape
    b = gate.shape[0]
    tm = _tile(seq, 128, 8)
    steps_per_batch = seq // tm
    row_block = pl.BlockSpec((tm, d), lambda i, p1, p2: (i, 0))
    per_batch = pl.BlockSpec((1, 1, d), lambda i, p1, p2: (i // steps_per_batch, 0, 0))
    out_specs = [row_block, row_block] if emit_x else [row_block]
    out_shape = [jax.ShapeDtypeStruct((m, d), norm_dtype)]
    if emit_x:
        out_shape.insert(0, jax.ShapeDtypeStruct((m, d), F32))
    return pl.pallas_call(
        functools.partial(_combine_kernel, tm=tm, emit_x=emit_x, col_chunks=min(8, d // LANES)),
        grid_spec=pltpu.PrefetchScalarGridSpec(
            num_scalar_prefetch=2,
            grid=(m // tm,),
            in_specs=[
                pl.BlockSpec(memory_space=pl.ANY),
                row_block,
                pl.BlockSpec((tm, LANES), lambda i, p1, p2: (i, 0)),
                per_batch,
                pl.BlockSpec((1, d), lambda i, p1, p2: (0, 0)),
                per_batch,
                per_batch,
            ],
            out_specs=out_specs,
            scratch_shapes=[pltpu.VMEM((2, 2, tm, d), F32), pltpu.VMEM((tm, d), F32),
                            pltpu.SemaphoreType.DMA((2,))],
        ),
        out_shape=out_shape,
        compiler_params=_params("arbitrary"),
        name="moe_combine",
    )(pos1, pos2, y_sorted, x2d, info, gate.reshape(b, 1, d), norm_g.reshape(1, d),
      norm_shift.reshape(b, 1, d), norm_scale.reshape(b, 1, d))


EXPERT_TILE = 256


def _hier_moe(x2d, seq, layer, ffn_norm, gate, next_norm, w_rg, b_rg, w_re, b_re, w_eg, w_eu, w_ed,
              norm_dtype, emit_x):
    m, d = x2d.shape
    n_layers, n_groups, e_per_group, _, f = w_eg.shape
    n_exp = n_groups * e_per_group
    tm = EXPERT_TILE
    max_tiles = (2 * m) // tm + n_exp
    h2d, info, cnt = _norm_router(x2d, *ffn_norm, seq, w_rg, b_rg, w_re, b_re, e_per_group)

    counts = cnt[0, :n_exp].astype(jnp.int32)
    tiles_per_expert = (counts + tm - 1) // tm
    tile_end = jnp.cumsum(tiles_per_expert)
    tile_start = tile_end - tiles_per_expert
    n_tiles = tile_end[-1:]
    expert = info[:, INFO_E1:INFO_E2 + 1].astype(jnp.int32)
    rank = info[:, INFO_R1:INFO_R2 + 1].astype(jnp.int32)
    pos = tile_start[expert] * tm + rank
    tile_ids = jnp.arange(max_tiles + 1, dtype=jnp.int32)
    tile_expert = jnp.sum((tile_ids[:, None] >= tile_end[None, :]).astype(jnp.int32), axis=1)
    last_expert = tile_expert[jnp.maximum(n_tiles[0] - 1, 0)]
    tile_expert = jnp.where(tile_ids < n_tiles[0], tile_expert, last_expert) + layer * n_exp
    token = jnp.broadcast_to(jnp.arange(m, dtype=jnp.int32)[:, None], (m, 2))
    sorted_token = jnp.zeros(((max_tiles + 1) * tm,), jnp.int32).at[pos.reshape(-1)].set(
        token.reshape(-1))

    y_sorted = _experts(h2d, tile_expert, n_tiles, sorted_token,
                        w_eg.reshape(n_layers * n_exp, d, f), w_eu.reshape(n_layers * n_exp, d, f),
                        w_ed.reshape(n_layers * n_exp, f, d), tm, max_tiles)
    return _moe_combine(y_sorted, pos[:, 0], pos[:, 1], x2d, info, gate, seq, *next_norm,
                        norm_dtype, emit_x)


def kernel(x, c, ada_w, ada_b, norm_mix_g, norm_ffn_g, w_in, b_forget, rel_bias, w_branch_a, w_branch_b, w_gate, b_gate, w_out, w_router_group, b_router_group, w_router_expert, b_router_expert, w_exp_gate, w_exp_up, w_exp_down, final_norm_g):
    b, s, d = x.shape
    n_layers = ada_w.shape[0]
    m = b * s
    heads_a, heads_b = b_forget.shape[1], rel_bias.shape[1]
    wa3 = 3 * heads_a * HEAD_DIM
    assert w_branch_a.shape[1] == heads_a * HEAD_DIM and w_branch_b.shape[1] == heads_b * HEAD_DIM
    assert w_in.shape[2] == wa3 + heads_a + 3 * heads_b * HEAD_DIM

    mod = _adaln_mod(c, ada_w, ada_b)
    mods = [[mod[l, :, i * d:(i + 1) * d] for i in range(6)] for l in range(n_layers)]
    w_in_t = jnp.swapaxes(w_in, 1, 2)
    zeros = jnp.zeros((b, d), F32)
    x2d = x.reshape(m, d)
    h2d = _norm_mod(x, norm_mix_g[0], mods[0][0], mods[0][1], BF16).reshape(m, d)
    for l in range(n_layers):
        _, _, gt_m, sh_f, sc_f, gt_f = mods[l]
        qkv = _in_projection(h2d, w_in_t, l, wa3, heads_a).reshape(b, s, -1)
        cum = _forget_cumsum(h2d.reshape(b, s, d), w_in_t, l, b_forget[l], wa3)
        ya = _fox_attention(qkv, cum, heads_a, (0, heads_a, 2 * heads_a))
        off = 3 * heads_a
        yb = _chunk_attention(qkv, rel_bias[l], (off, off + heads_b, off + 2 * heads_b))
        merged = _gated_merge(h2d, ya.reshape(m, -1), yb.reshape(m, -1), l, w_gate, b_gate[l],
                              w_branch_a, w_branch_b)
        x2d = _out_projection(merged, w_out, l, x2d, gt_m, s)
        last = l + 1 == n_layers
        if last:
            next_norm, norm_dtype = (final_norm_g, zeros, zeros), x.dtype
        else:
            next_norm, norm_dtype = (norm_mix_g[l + 1], mods[l + 1][0], mods[l + 1][1]), BF16
        outs = _hier_moe(x2d, s, l, (norm_ffn_g[l], sh_f, sc_f), gt_f, next_norm,
                         w_router_group[l], b_router_group[l], w_router_expert[l],
                         b_router_expert[l], w_exp_gate, w_exp_up, w_exp_down,
                         norm_dtype, emit_x=not last)
        if not last:
            x2d, h2d = outs
    return outs[0].reshape(b, s, d)
```

```python
import functools

import jax
import jax.numpy as jnp
from jax import lax
from jax.experimental import pallas as pl
from jax.experimental.pallas import tpu as pltpu

CHUNK = 64
LEFT_CHUNKS = 8
EPS = 1e-6
NEG_INF = -1e30
LANES = 128
HEAD_DIM = 128
LOG2E = 1.4426950408889634
V7X_VMEM_LIMIT_BYTES = 60000 * 1024

F32 = jnp.float32
BF16 = jnp.bfloat16


def _params(*sem):
    return pltpu.CompilerParams(dimension_semantics=sem, vmem_limit_bytes=V7X_VMEM_LIMIT_BYTES)


def _tile(n, pref, mult=LANES):
    t = (min(pref, n) // mult) * mult
    while t >= mult:
        if n % t == 0:
            return t
        t -= mult
    return n


def _cast_rows(src_ref, dst_ref, rows_per_step):
    n = src_ref.shape[0]

    def body(r, carry):
        rows = pl.ds(pl.multiple_of(r * rows_per_step, rows_per_step), rows_per_step)
        dst_ref[rows, :] = src_ref[rows, :].astype(dst_ref.dtype)
        return carry

    lax.fori_loop(0, n // rows_per_step, body, 0)


def _mod_kernel(c_ref, w_ref, b_ref, o_ref):
    c = c_ref[...]
    c_act = (c * jax.nn.sigmoid(c)).astype(BF16)
    o_ref[0] = jnp.dot(c_act, w_ref[0].astype(BF16), preferred_element_type=F32) + b_ref[0]


def _adaln_mod(c, ada_w, ada_b):
    n_layers, d, n = ada_w.shape
    b = c.shape[0]
    bp = -(-b // 8) * 8
    c_pad = jnp.pad(c, ((0, bp - b), (0, 0)))
    tn = _tile(n, 512)
    out = pl.pallas_call(
        _mod_kernel,
        grid=(n_layers, n // tn),
        in_specs=[
            pl.BlockSpec((bp, d), lambda l, j: (0, 0)),
            pl.BlockSpec((1, d, tn), lambda l, j: (l, 0, j)),
            pl.BlockSpec((1, 1, tn), lambda l, j: (l, 0, j)),
        ],
        out_specs=pl.BlockSpec((1, bp, tn), lambda l, j: (l, 0, j)),
        out_shape=jax.ShapeDtypeStruct((n_layers, bp, n), F32),
        compiler_params=_params("arbitrary", "arbitrary"),
        name="adaln_mod",
    )(c_pad, ada_w, ada_b.reshape(n_layers, 1, n))
    return out[:, :b]


def _norm_mod_kernel(x_ref, g_ref, sh_ref, sc_ref, o_ref):
    x = x_ref[0]
    ms = jnp.mean(x * x, axis=-1, keepdims=True)
    y = x * lax.rsqrt(ms + EPS) * g_ref[...]
    o_ref[0] = (y * (1.0 + sc_ref[0]) + sh_ref[0]).astype(o_ref.dtype)


def _norm_mod(x, g, shift, scale, out_dtype):
    b, s, d = x.shape
    ts = _tile(s, 256, 8)
    return pl.pallas_call(
        _norm_mod_kernel,
        grid=(b, s // ts),
        in_specs=[
            pl.BlockSpec((1, ts, d), lambda i, j: (i, j, 0)),
            pl.BlockSpec((1, d), lambda i, j: (0, 0)),
            pl.BlockSpec((1, 1, d), lambda i, j: (i, 0, 0)),
            pl.BlockSpec((1, 1, d), lambda i, j: (i, 0, 0)),
        ],
        out_specs=pl.BlockSpec((1, ts, d), lambda i, j: (i, j, 0)),
        out_shape=jax.ShapeDtypeStruct((b, s, d), out_dtype),
        compiler_params=_params("arbitrary", "arbitrary"),
        name="norm_mod",
    )(x, g.reshape(1, d), shift.reshape(b, 1, d), scale.reshape(b, 1, d))


def _dot_nt(a, b):
    return lax.dot_general(a, b, (((1,), (1,)), ((), ())), preferred_element_type=F32)


def _inproj_kernel(h_ref, w_ref, o_ref, wbf_ref, *, rows_per_step):
    @pl.when(pl.program_id(1) == 0)
    def _():
        _cast_rows(w_ref, wbf_ref, rows_per_step)

    o_ref[...] = _dot_nt(h_ref[...], wbf_ref[...]).astype(o_ref.dtype)


def _in_projection(h2d, w_in_t, layer, w_a3, n_forget):
    m, d = h2d.shape
    n_out = w_in_t.shape[1] - n_forget
    tn = _tile(w_a3, 512)
    assert n_out % tn == 0 and w_a3 % tn == 0 and n_forget % 8 == 0
    tm = _tile(m, 1024, 8)
    n_plain = w_a3 // tn
    kern = functools.partial(_inproj_kernel, rows_per_step=_tile(tn, 128, 8))
    return pl.pallas_call(
        kern,
        grid=(n_out // tn, m // tm),
        in_specs=[
            pl.BlockSpec((tm, d), lambda j, i: (i, 0)),
            pl.BlockSpec((None, pl.Element(tn), pl.Element(d)),
                         lambda j, i: (layer, pl.multiple_of(
                             j * tn + jnp.where(j >= n_plain, n_forget, 0), 8), 0)),
        ],
        out_specs=pl.BlockSpec((tm, tn), lambda j, i: (i, j)),
        out_shape=jax.ShapeDtypeStruct((m, n_out), BF16),
        scratch_shapes=[pltpu.VMEM((tn, d), BF16)],
        compiler_params=_params("arbitrary", "arbitrary"),
        name="in_projection",
    )(h2d, w_in_t)


def _split3_dot(x, u):
    x1 = x.astype(BF16)
    r1 = x - x1.astype(F32)
    x2 = r1.astype(BF16)
    x3 = (r1 - x2.astype(F32)).astype(BF16)
    ub = u.astype(BF16)
    return (jnp.dot(x1, ub, preferred_element_type=F32)
            + jnp.dot(x2, ub, preferred_element_type=F32)
            + jnp.dot(x3, ub, preferred_element_type=F32))


def _forget_kernel(h_ref, w_ref, b_ref, o_ref, carry_ref):
    s = pl.program_id(1)

    @pl.when(s == 0)
    def _():
        carry_ref[...] = jnp.zeros_like(carry_ref)

    ts = h_ref.shape[1]
    f = _dot_nt(w_ref[...].astype(BF16), h_ref[0]) + b_ref[...]
    log_f = jnp.minimum(f, 0.0) - jnp.log1p(jnp.exp(-jnp.abs(f)))
    row = lax.broadcasted_iota(jnp.int32, (ts, ts), 0)
    col = lax.broadcasted_iota(jnp.int32, (ts, ts), 1)
    upper = jnp.where(row <= col, 1.0, 0.0).astype(F32)
    cs = _split3_dot(log_f, upper) + carry_ref[:, 0:1]
    o_ref[0] = cs * LOG2E
    carry_ref[...] = jnp.broadcast_to(cs[:, ts - 1:ts], carry_ref.shape)


def _forget_cumsum(h, w_in_t, layer, b_forget, row0):
    b, s, d = h.shape
    n_heads = b_forget.shape[0]
    hp = -(-n_heads // 8) * 8
    assert row0 % 8 == 0
    ts = _tile(s, 512)
    bias = jnp.pad(b_forget.astype(F32), (0, hp - n_heads)).reshape(hp, 1)
    return pl.pallas_call(
        _forget_kernel,
        grid=(b, s // ts),
        in_specs=[
            pl.BlockSpec((1, ts, d), lambda i, j: (i, j, 0)),
            pl.BlockSpec((None, pl.Element(hp), pl.Element(d)), lambda i, j: (layer, row0, 0)),
            pl.BlockSpec((hp, 1), lambda i, j: (0, 0)),
        ],
        out_specs=pl.BlockSpec((1, hp, ts), lambda i, j: (i, 0, j)),
        out_shape=jax.ShapeDtypeStruct((b, hp, s), F32),
        scratch_shapes=[pltpu.VMEM((hp, LANES), F32)],
        compiler_params=_params("arbitrary", "arbitrary"),
        name="forget_cumsum",
    )(h, w_in_t, bias)


def _fox_kernel(q_ref, k_ref, v_ref, cum_ref, o_ref, *, tq, scale_log2e):
    head = pl.program_id(1)
    seq = q_ref.shape[1]
    row = lax.broadcasted_iota(jnp.int32, (tq, tq), 0)
    col = lax.broadcasted_iota(jnp.int32, (tq, tq), 1)
    causal = row >= col
    decay = cum_ref[0, pl.ds(head, 1), :]

    n_blocks = seq // tq

    def scores(iq):
        q0 = iq * tq
        q = q_ref[0, q0:q0 + tq, :]
        s_diag = _dot_nt(q, k_ref[0, q0:q0 + tq, :]) * scale_log2e - decay[:, q0:q0 + tq]
        s_diag = jnp.where(causal, s_diag, NEG_INF)
        s_past = None
        if iq > 0:
            s_past = _dot_nt(q, k_ref[0, 0:q0, :]) * scale_log2e - decay[:, 0:q0]
        return s_diag, s_past

    nxt = scores(0)
    for iq in range(n_blocks):
        q0 = iq * tq
        s_diag, s_past = nxt
        if iq + 1 < n_blocks:
            nxt = scores(iq + 1)
        m = jnp.max(s_diag, axis=-1, keepdims=True)
        if iq > 0:
            m = jnp.maximum(m, jnp.max(s_past, axis=-1, keepdims=True))
        p_diag = jnp.exp2(s_diag - m)
        l = jnp.sum(p_diag, axis=-1, keepdims=True)
        acc = jnp.dot(p_diag.astype(BF16), v_ref[0, q0:q0 + tq, :], preferred_element_type=F32)
        if iq > 0:
            p_past = jnp.exp2(s_past - m)
            l = l + jnp.sum(p_past, axis=-1, keepdims=True)
            acc = acc + jnp.dot(p_past.astype(BF16), v_ref[0, 0:q0, :], preferred_element_type=F32)
        o_ref[0, q0:q0 + tq, :] = (acc / l).astype(o_ref.dtype)


def _fox_attention(qkv, cum, n_heads, col_blocks):
    b, s, _ = qkv.shape
    hp = cum.shape[1]
    tq = _tile(s, 256)
    qc, kc, vc = col_blocks
    kern = functools.partial(_fox_kernel, tq=tq, scale_log2e=HEAD_DIM ** -0.5 * LOG2E)
    return pl.pallas_call(
        kern,
        grid=(b, n_heads),
        in_specs=[
            pl.BlockSpec((1, s, HEAD_DIM), lambda i, h: (i, 0, qc + h)),
            pl.BlockSpec((1, s, HEAD_DIM), lambda i, h: (i, 0, kc + h)),
            pl.BlockSpec((1, s, HEAD_DIM), lambda i, h: (i, 0, vc + h)),
            pl.BlockSpec((1, hp, s), lambda i, h: (i, 0, 0)),
        ],
        out_specs=pl.BlockSpec((1, s, HEAD_DIM), lambda i, h: (i, 0, h)),
        out_shape=jax.ShapeDtypeStruct((b, s, n_heads * HEAD_DIM), BF16),
        compiler_params=_params("arbitrary", "arbitrary"),
        name="fox_attention",
    )(qkv, qkv, qkv, cum)


Q_CHUNKS = 2
QB = Q_CHUNKS * CHUNK
PAD = LEFT_CHUNKS * CHUNK
WIN = (LEFT_CHUNKS + Q_CHUNKS) * CHUNK
CHUNK_SHIFT = CHUNK.bit_length() - 1


def _chunk_bias_kernel(tab_ref, o_ref, *, max_rel, c0):
    h = pl.program_id(0)

    def visible(qi, kj):
        q_chunk = jnp.right_shift(qi, CHUNK_SHIFT)
        k_chunk = jnp.right_shift(kj, CHUNK_SHIFT)
        return jnp.logical_and(k_chunk >= q_chunk, k_chunk <= q_chunk + LEFT_CHUNKS)

    if c0 > 0:
        qi = lax.broadcasted_iota(jnp.int32, (QB, c0), 0)
        kj = lax.broadcasted_iota(jnp.int32, (QB, c0), 1)
        far = tab_ref[h, 2 * max_rel] * LOG2E
        o_ref[0, :, 0:c0] = jnp.where(visible(qi, kj), far, NEG_INF)
    qi = lax.broadcasted_iota(jnp.int32, (QB, WIN - c0), 0)
    kj = lax.broadcasted_iota(jnp.int32, (QB, WIN - c0), 1) + c0
    idx = jnp.clip(PAD + qi - kj, -max_rel, max_rel) + max_rel

    def body(r, acc):
        return jnp.where(idx == r, tab_ref[h, r], acc)

    near = lax.fori_loop(0, 2 * max_rel + 1, body, jnp.zeros((QB, WIN - c0), F32), unroll=8)
    o_ref[0, :, c0:WIN] = jnp.where(visible(qi, kj), near * LOG2E, NEG_INF)


def _chunk_bias(rel_table):
    n_heads, n_rel = rel_table.shape
    max_rel = (n_rel - 1) // 2
    c0 = max(0, (PAD - max_rel + 1) // LANES * LANES)
    return pl.pallas_call(
        functools.partial(_chunk_bias_kernel, max_rel=max_rel, c0=c0),
        grid=(n_heads,),
        in_specs=[pl.BlockSpec(memory_space=pltpu.SMEM)],
        out_specs=pl.BlockSpec((1, QB, WIN), lambda h: (h, 0, 0)),
        out_shape=jax.ShapeDtypeStruct((n_heads, QB, WIN), F32),
        compiler_params=_params("arbitrary"),
        name="chunk_bias",
    )(rel_table.astype(F32))


def _chunk_kernel(q_ref, k_ref, v_ref, bias_ref, o_ref, *, scale_log2e):
    seq = q_ref.shape[1]
    n_blocks = seq // QB

    def scores(i):
        q0 = i * QB
        k_lo = max(0, q0 - PAD)
        width = q0 + QB - k_lo
        q = q_ref[0, q0:q0 + QB, :]
        return _dot_nt(q, k_ref[0, k_lo:q0 + QB, :]) * scale_log2e + bias_ref[0, :, WIN - width:WIN]

    s_next = scores(0)
    for i in range(n_blocks):
        q0 = i * QB
        k_lo = max(0, q0 - PAD)
        s = s_next
        if i + 1 < n_blocks:
            s_next = scores(i + 1)
        m = jnp.max(s, axis=-1, keepdims=True)
        p = jnp.exp2(s - m)
        l = jnp.sum(p, axis=-1, keepdims=True)
        out = jnp.dot(p.astype(BF16), v_ref[0, k_lo:q0 + QB, :], preferred_element_type=F32)
        o_ref[0, q0:q0 + QB, :] = (out / l).astype(o_ref.dtype)


def _chunk_attention(qkv, rel_table, col_blocks):
    b, s, _ = qkv.shape
    n_heads = rel_table.shape[0]
    assert s % QB == 0
    qc, kc, vc = col_blocks
    kern = functools.partial(_chunk_kernel, scale_log2e=HEAD_DIM ** -0.5 * LOG2E)
    return pl.pallas_call(
        kern,
        grid=(b, n_heads),
        in_specs=[
            pl.BlockSpec((1, s, HEAD_DIM), lambda i, h: (i, 0, qc + h)),
            pl.BlockSpec((1, s, HEAD_DIM), lambda i, h: (i, 0, kc + h)),
            pl.BlockSpec((1, s, HEAD_DIM), lambda i, h: (i, 0, vc + h)),
            pl.BlockSpec((1, QB, WIN), lambda i, h: (h, 0, 0)),
        ],
        out_specs=pl.BlockSpec((1, s, HEAD_DIM), lambda i, h: (i, 0, h)),
        out_shape=jax.ShapeDtypeStruct((b, s, n_heads * HEAD_DIM), BF16),
        compiler_params=_params("arbitrary", "arbitrary"),
        name="chunk_attention",
    )(qkv, qkv, qkv, _chunk_bias(rel_table))


def _merge_kernel(h_ref, ya_ref, yb_ref, wga_ref, wgb_ref, wa_ref, wb_ref, bga_ref, bgb_ref, o_ref,
                  wga_s, wgb_s, wa_s, wb_s, *, rows_per_step):
    @pl.when(pl.program_id(1) == 0)
    def _():
        _cast_rows(wga_ref, wga_s, rows_per_step)
        _cast_rows(wgb_ref, wgb_s, rows_per_step)
        _cast_rows(wa_ref, wa_s, rows_per_step)
        _cast_rows(wb_ref, wb_s, rows_per_step)

    h = h_ref[...]
    gate_a = jax.nn.sigmoid(jnp.dot(h, wga_s[...], preferred_element_type=F32) + bga_ref[...])
    gate_b = jax.nn.sigmoid(jnp.dot(h, wgb_s[...], preferred_element_type=F32) + bgb_ref[...])
    proj_a = jnp.dot(ya_ref[...], wa_s[...], preferred_element_type=F32)
    proj_b = jnp.dot(yb_ref[...], wb_s[...], preferred_element_type=F32)
    o_ref[...] = (gate_a * proj_a + gate_b * proj_b).astype(o_ref.dtype)


def _gated_merge(h2d, ya2d, yb2d, layer, w_gate, b_gate, w_a, w_b):
    m, d = h2d.shape
    wa_rows, wb_rows = w_a.shape[1], w_b.shape[1]
    tn = _tile(d, 256)
    tm = _tile(m, 1024, 8)
    nb = d // tn
    single = pl.Buffered(1)
    kern = functools.partial(_merge_kernel, rows_per_step=_tile(min(d, wa_rows, wb_rows), 256, 8))
    b2 = b_gate.reshape(1, 2 * d)
    return pl.pallas_call(
        kern,
        grid=(nb, m // tm),
        in_specs=[
            pl.BlockSpec((tm, d), lambda j, i: (i, 0)),
            pl.BlockSpec((tm, wa_rows), lambda j, i: (i, 0)),
            pl.BlockSpec((tm, wb_rows), lambda j, i: (i, 0)),
            pl.BlockSpec((None, d, tn), lambda j, i: (layer, 0, j), pipeline_mode=single),
            pl.BlockSpec((None, d, tn), lambda j, i: (layer, 0, nb + j), pipeline_mode=single),
            pl.BlockSpec((None, wa_rows, tn), lambda j, i: (layer, 0, j), pipeline_mode=single),
            pl.BlockSpec((None, wb_rows, tn), lambda j, i: (layer, 0, j), pipeline_mode=single),
            pl.BlockSpec((1, tn), lambda j, i: (0, j)),
            pl.BlockSpec((1, tn), lambda j, i: (0, nb + j)),
        ],
        out_specs=pl.BlockSpec((tm, tn), lambda j, i: (i, j)),
        out_shape=jax.ShapeDtypeStruct((m, d), BF16),
        scratch_shapes=[pltpu.VMEM((d, tn), BF16), pltpu.VMEM((d, tn), BF16),
                        pltpu.VMEM((wa_rows, tn), BF16), pltpu.VMEM((wb_rows, tn), BF16)],
        compiler_params=_params("arbitrary", "arbitrary"),
        name="gated_merge",
    )(h2d, ya2d, yb2d, w_gate, w_gate, w_a, w_b, b2, b2)


def _outproj_kernel(a_ref, w_ref, x_ref, gt_ref, o_ref, w_s, *, rows_per_step):
    @pl.when(pl.program_id(1) == 0)
    def _():
        _cast_rows(w_ref, w_s, rows_per_step)

    y = jnp.dot(a_ref[...], w_s[...], preferred_element_type=F32)
    o_ref[...] = x_ref[...] + gt_ref[0] * y


def _out_projection(a2d, w_out, layer, x2d, gate, seq):
    m, k = a2d.shape
    d = w_out.shape[2]
    b = gate.shape[0]
    tn = _tile(d, 512)
    tm = _tile(seq, 1024, 8)
    steps_per_batch = seq // tm
    kern = functools.partial(_outproj_kernel, rows_per_step=_tile(k, 256, 8))
    return pl.pallas_call(
        kern,
        grid=(d // tn, m // tm),
        in_specs=[
            pl.BlockSpec((tm, k), lambda j, i: (i, 0)),
            pl.BlockSpec((None, k, tn), lambda j, i: (layer, 0, j), pipeline_mode=pl.Buffered(1)),
            pl.BlockSpec((tm, tn), lambda j, i: (i, j)),
            pl.BlockSpec((1, 1, tn), lambda j, i: (i // steps_per_batch, 0, j)),
        ],
        out_specs=pl.BlockSpec((tm, tn), lambda j, i: (i, j)),
        out_shape=jax.ShapeDtypeStruct((m, d), F32),
        scratch_shapes=[pltpu.VMEM((k, tn), BF16)],
        compiler_params=_params("arbitrary", "arbitrary"),
        name="out_projection",
    )(a2d, w_out, x2d, gate.reshape(b, 1, d))


INFO_E1, INFO_E2, INFO_W1, INFO_W2, INFO_R1, INFO_R2 = range(6)


def _router_kernel(x_ref, g_ref, sh_ref, sc_ref, w_ref, b_ref, h_ref, info_ref, cnt_ref, carry_ref,
                   *, n_groups, e_per_group):
    @pl.when(pl.program_id(0) == 0)
    def _():
        carry_ref[...] = jnp.zeros_like(carry_ref)

    tm = x_ref.shape[0]
    x = x_ref[...]
    ms = jnp.mean(x * x, axis=-1, keepdims=True)
    h = x * lax.rsqrt(ms + EPS) * g_ref[...] * (1.0 + sc_ref[0]) + sh_ref[0]
    _store_slabs(h_ref, h)
    logits = jnp.dot(h.astype(BF16), w_ref[...].astype(BF16),
                     preferred_element_type=F32) + b_ref[...]
    lane = lax.broadcasted_iota(jnp.int32, (tm, LANES), 1).astype(F32)
    big = float(LANES)
    is_group = lane < n_groups
    gl = jnp.where(is_group, logits, -jnp.inf)
    g_max = jnp.max(gl, axis=-1, keepdims=True)
    g_sel = jnp.min(jnp.where(gl == g_max, lane, big), axis=-1, keepdims=True)
    g_w = 1.0 / jnp.sum(jnp.where(is_group, jnp.exp(gl - g_max), 0.0), axis=-1, keepdims=True)

    expert = lane - n_groups
    lo = g_sel * e_per_group
    in_group = jnp.logical_and(expert >= lo, expert < lo + e_per_group)
    el = jnp.where(in_group, logits, -jnp.inf)
    v1 = jnp.max(el, axis=-1, keepdims=True)
    e1 = jnp.min(jnp.where(el == v1, expert, big), axis=-1, keepdims=True)
    el2 = jnp.where(expert == e1, -jnp.inf, el)
    v2 = jnp.max(el2, axis=-1, keepdims=True)
    e2 = jnp.min(jnp.where(el2 == v2, expert, big), axis=-1, keepdims=True)
    t = jnp.exp(v2 - v1)
    w1 = g_w / (1.0 + t)
    w2 = g_w * t / (1.0 + t)

    onehot = jnp.where(jnp.logical_or(lane == e1, lane == e2), 1.0, 0.0)
    row = lax.broadcasted_iota(jnp.int32, (tm, tm), 0)
    col = lax.broadcasted_iota(jnp.int32, (tm, tm), 1)
    before = jnp.where(col < row, 1.0, 0.0).astype(BF16)
    prior = jnp.dot(before, onehot.astype(BF16), preferred_element_type=F32) + carry_ref[0:1, :]
    r1 = jnp.sum(jnp.where(lane == e1, prior, 0.0), axis=-1, keepdims=True)
    r2 = jnp.sum(jnp.where(lane == e2, prior, 0.0), axis=-1, keepdims=True)
    total = carry_ref[0:1, :] + jnp.sum(onehot, axis=0, keepdims=True)
    carry_ref[...] = jnp.broadcast_to(total, carry_ref.shape)
    cnt_ref[...] = jnp.broadcast_to(total, cnt_ref.shape)

    info = jnp.zeros((tm, LANES), F32)
    for idx, val in ((INFO_E1, e1), (INFO_E2, e2), (INFO_W1, w1), (INFO_W2, w2),
                     (INFO_R1, r1), (INFO_R2, r2)):
        info = jnp.where(lane == idx, val, info)
    info_ref[...] = info


def _norm_router(x2d, g, shift, scale, seq, w_rg, b_rg, w_re, b_re, e_per_group):
    m, d = x2d.shape
    pitch = _slab_pitch(d)
    b = shift.shape[0]
    n_groups = w_rg.shape[1]
    n_exp = w_re.shape[1]
    assert n_groups + n_exp <= LANES
    pad = LANES - n_groups - n_exp
    w = jnp.pad(jnp.concatenate([w_rg, w_re], axis=1), ((0, 0), (0, pad)))
    bias = jnp.pad(jnp.concatenate([b_rg, b_re]).astype(F32), (0, pad)).reshape(1, LANES)
    tm = _tile(seq, 256, 8)
    steps_per_batch = seq // tm
    kern = functools.partial(_router_kernel, n_groups=n_groups, e_per_group=e_per_group)
    return pl.pallas_call(
        kern,
        grid=(m // tm,),
        in_specs=[
            pl.BlockSpec((tm, d), lambda i: (i, 0)),
            pl.BlockSpec((1, d), lambda i: (0, 0)),
            pl.BlockSpec((1, 1, d), lambda i: (i // steps_per_batch, 0, 0)),
            pl.BlockSpec((1, 1, d), lambda i: (i // steps_per_batch, 0, 0)),
            pl.BlockSpec((d, LANES), lambda i: (0, 0)),
            pl.BlockSpec((1, LANES), lambda i: (0, 0)),
        ],
        out_specs=[pl.BlockSpec((tm * pitch, LANES), lambda i: (i, 0)),
                   pl.BlockSpec((tm, LANES), lambda i: (i, 0)),
                   pl.BlockSpec((8, LANES), lambda i: (0, 0))],
        out_shape=[jax.ShapeDtypeStruct((m * pitch, LANES), F32),
                   jax.ShapeDtypeStruct((m, LANES), F32), jax.ShapeDtypeStruct((8, LANES), F32)],
        scratch_shapes=[pltpu.VMEM((8, LANES), F32)],
        compiler_params=_params("arbitrary"),
        name="moe_norm_router",
    )(x2d, g.reshape(1, d), shift.reshape(b, 1, d), scale.reshape(b, 1, d), w, bias)


def _slab_rows(d):
    return d // LANES


def _slab_pitch(d):
    pitch = -(-_slab_rows(d) // 8) * 8
    return pitch if (pitch // 8) % 2 else pitch + 8


def _store_slabs(ref, val):
    n, d = val.shape
    pitch = ref.shape[0] // n
    for s in range(pitch):
        if s < _slab_rows(d):
            piece = val[:, s * LANES:(s + 1) * LANES]
        else:
            piece = jnp.zeros((n, LANES), val.dtype)
        ref[pl.ds(s, n, stride=pitch), :] = piece


def _load_slabs(ref, n, pitch, s0, s1):
    return jnp.concatenate([ref[pl.ds(s, n, stride=pitch), :] for s in range(s0, s1)], axis=1)


def _slab_copy(src_hbm, src_row, dst_ref, dst_row, rows, sem):
    return pltpu.make_async_copy(src_hbm.at[pl.ds(pl.multiple_of(src_row, 8), rows)],
                                 dst_ref.at[pl.ds(pl.multiple_of(dst_row, 8), rows)], sem)


def _gather_slabs(idx_ref, base, n, src_hbm, dst_ref, pitch, rows, sem):
    def body(r, carry):
        _slab_copy(src_hbm, idx_ref[base + r], dst_ref, r * pitch, rows, sem).start()
        return carry

    lax.fori_loop(0, n, body, 0, unroll=8)


def _wait_slabs(n, src_hbm, dst_ref, rows, sem):
    pltpu.make_async_copy(src_hbm.at[pl.ds(0, n * rows)], dst_ref.at[pl.ds(0, n * rows)], sem).wait()


def _expert_changed(te_ref, i):
    return jnp.logical_or(i == 0, te_ref[i] != te_ref[jnp.maximum(i - 1, 0)])


def _expert_up_kernel(te_ref, nt_ref, tok_ref, h_hbm, wg_ref, wu_ref, a_ref, xbuf, wg_s, wu_s, sem,
                      *, tm, rows_per_step, k_chunks):
    i = pl.program_id(0)
    n_tiles = nt_ref[0]
    d = wg_ref.shape[0]
    rows = _slab_rows(d)
    pitch = xbuf.shape[1] // tm
    kc = d // k_chunks
    issue_chunks = max(1, k_chunks // 2)
    rows_per_chunk = tm // issue_chunks

    @pl.when(jnp.logical_and(i < n_tiles, _expert_changed(te_ref, i)))
    def _():
        _cast_rows(wg_ref, wg_s, rows_per_step)
        _cast_rows(wu_ref, wu_s, rows_per_step)

    @pl.when(i == 0)
    def _():
        _gather_slabs(tok_ref, 0, tm, h_hbm, xbuf.at[0], pitch, rows, sem.at[0])

    @pl.when(i < n_tiles)
    def _():
        slot = i % 2
        nxt = 1 - slot
        _wait_slabs(tm, h_hbm, xbuf.at[slot], rows, sem.at[slot])
        g = u = None
        for c in range(k_chunks):
            if c < issue_chunks:
                for r in range(c * rows_per_chunk, (c + 1) * rows_per_chunk):
                    _slab_copy(h_hbm, tok_ref[(i + 1) * tm + r], xbuf.at[nxt], r * pitch, rows,
                               sem.at[nxt]).start()
            x = _load_slabs(xbuf.at[slot], tm, pitch, c * kc // LANES, (c + 1) * kc // LANES)
            x = x.astype(BF16)
            gc = jnp.dot(x, wg_s[c * kc:(c + 1) * kc, :], preferred_element_type=F32)
            uc = jnp.dot(x, wu_s[c * kc:(c + 1) * kc, :], preferred_element_type=F32)
            g = gc if g is None else g + gc
            u = uc if u is None else u + uc
        a_ref[...] = (g * jax.nn.sigmoid(g) * u).astype(a_ref.dtype)

    @pl.when(i == n_tiles)
    def _():
        _wait_slabs(tm, h_hbm, xbuf.at[i % 2], rows, sem.at[i % 2])

    @pl.when(i >= n_tiles)
    def _():
        a_ref[...] = jnp.zeros_like(a_ref)


def _expert_down_kernel(te_ref, nt_ref, a_ref, wd_ref, y_ref, wd_s, *, rows_per_step):
    i = pl.program_id(0)

    @pl.when(jnp.logical_and(i < nt_ref[0], _expert_changed(te_ref, i)))
    def _():
        _cast_rows(wd_ref, wd_s, rows_per_step)

    @pl.when(i < nt_ref[0])
    def _():
        _store_slabs(y_ref, jnp.dot(a_ref[...], wd_s[...], preferred_element_type=F32))

    @pl.when(i >= nt_ref[0])
    def _():
        y_ref[...] = jnp.zeros_like(y_ref)


def _experts(h_slabs, tile_expert, n_tiles, sorted_row, w_gate, w_up, w_down, tm, max_tiles):
    n_exp, d, f = w_gate.shape
    pitch = _slab_pitch(d)
    k_chunks = min(8, d // LANES)
    assert d % (k_chunks * LANES) == 0 and tm % k_chunks == 0
    a = pl.pallas_call(
        functools.partial(_expert_up_kernel, tm=tm, rows_per_step=_tile(d, 256, 8),
                          k_chunks=k_chunks),
        grid_spec=pltpu.PrefetchScalarGridSpec(
            num_scalar_prefetch=3,
            grid=(max_tiles + 1,),
            in_specs=[
                pl.BlockSpec(memory_space=pl.ANY),
                pl.BlockSpec((None, d, f), lambda i, te, nt, tok: (te[i], 0, 0)),
                pl.BlockSpec((None, d, f), lambda i, te, nt, tok: (te[i], 0, 0)),
            ],
            out_specs=pl.BlockSpec((tm, f), lambda i, te, nt, tok: (i, 0)),
            scratch_shapes=[pltpu.VMEM((2, tm * pitch, LANES), F32), pltpu.VMEM((d, f), BF16),
                            pltpu.VMEM((d, f), BF16), pltpu.SemaphoreType.DMA((2,))],
        ),
        out_shape=jax.ShapeDtypeStruct(((max_tiles + 1) * tm, f), BF16),
        compiler_params=_params("arbitrary"),
        name="moe_expert_up",
    )(tile_expert, n_tiles, sorted_row, h_slabs, w_gate, w_up)
    return pl.pallas_call(
        functools.partial(_expert_down_kernel, rows_per_step=_tile(f, 64, 8)),
        grid_spec=pltpu.PrefetchScalarGridSpec(
            num_scalar_prefetch=2,
            grid=(max_tiles,),
            in_specs=[
                pl.BlockSpec((tm, f), lambda i, te, nt: (i, 0)),
                pl.BlockSpec((None, f, d), lambda i, te, nt: (te[i], 0, 0)),
            ],
            out_specs=pl.BlockSpec((tm * pitch, LANES), lambda i, te, nt: (i, 0)),
            scratch_shapes=[pltpu.VMEM((f, d), BF16)],
        ),
        out_shape=jax.ShapeDtypeStruct((max_tiles * tm * pitch, LANES), F32),
        compiler_params=_params("arbitrary"),
        name="moe_expert_down",
    )(tile_expert, n_tiles, a, w_down)


def _combine_kernel(p1_ref, p2_ref, y_hbm, x_ref, info_ref, gt_ref, g_ref, sh_ref, sc_ref, *refs,
                    tm, emit_x, col_chunks):
    if emit_x:
        x_out, h_out, ybuf, xs, sem = refs
    else:
        h_out, ybuf, xs, sem = refs
    i = pl.program_id(0)
    n_steps = pl.num_programs(0)
    d = x_ref.shape[1]
    rows = _slab_rows(d)
    pitch = ybuf.shape[2] // tm
    cw = d // col_chunks
    issue_chunks = max(1, col_chunks // 2)
    rows_per_chunk = tm // issue_chunks

    @pl.when(i == 0)
    def _():
        _gather_slabs(p1_ref, 0, tm, y_hbm, ybuf.at[0, 0], pitch, rows, sem.at[0])
        _gather_slabs(p2_ref, 0, tm, y_hbm, ybuf.at[0, 1], pitch, rows, sem.at[0])

    slot = i % 2
    nxt = 1 - slot
    nxt_base = jnp.minimum(i + 1, n_steps - 1) * tm
    _wait_slabs(tm, y_hbm, ybuf.at[slot, 0], rows, sem.at[slot])
    _wait_slabs(tm, y_hbm, ybuf.at[slot, 1], rows, sem.at[slot])
    info = info_ref[...]
    w1 = info[:, INFO_W1:INFO_W1 + 1]
    w2 = info[:, INFO_W2:INFO_W2 + 1]
    ssq = jnp.zeros((tm, 1), F32)
    for c in range(col_chunks):
        if c < issue_chunks:
            for r in range(c * rows_per_chunk, (c + 1) * rows_per_chunk):
                for k, p_ref in enumerate((p1_ref, p2_ref)):
                    _slab_copy(y_hbm, p_ref[nxt_base + r], ybuf.at[nxt, k], r * pitch, rows,
                               sem.at[nxt]).start()
        cols = slice(c * cw, (c + 1) * cw)
        s0, s1 = c * cw // LANES, (c + 1) * cw // LANES
        y = (w1 * _load_slabs(ybuf.at[slot, 0], tm, pitch, s0, s1)
             + w2 * _load_slabs(ybuf.at[slot, 1], tm, pitch, s0, s1))
        xc = x_ref[:, cols] + gt_ref[0][:, cols] * y
        xs[:, cols] = xc
        ssq = ssq + jnp.sum(xc * xc, axis=-1, keepdims=True)
    x = xs[...]
    if emit_x:
        x_out[...] = x
    h = x * lax.rsqrt(ssq * (1.0 / d) + EPS) * g_ref[...] * (1.0 + sc_ref[0]) + sh_ref[0]
    h_out[...] = h.astype(h_out.dtype)

    @pl.when(i == n_steps - 1)
    def _():
        _wait_slabs(tm, y_hbm, ybuf.at[nxt, 0], rows, sem.at[nxt])
        _wait_slabs(tm, y_hbm, ybuf.at[nxt, 1], rows, sem.at[nxt])


def _moe_combine(y_sorted, pos1, pos2, x2d, info, gate, seq, norm_g, norm_shift, norm_scale,
                 norm_dtype, emit_x):
    m, d = x2d.shape
    b = gate.shape[0]
    tm = _tile(seq, 128, 8)
    steps_per_batch = seq // tm
    row_block = pl.BlockSpec((tm, d), lambda i, p1, p2: (i, 0))
    per_batch = pl.BlockSpec((1, 1, d), lambda i, p1, p2: (i // steps_per_batch, 0, 0))
    out_specs = [row_block, row_block] if emit_x else [row_block]
    out_shape = [jax.ShapeDtypeStruct((m, d), norm_dtype)]
    if emit_x:
        out_shape.insert(0, jax.ShapeDtypeStruct((m, d), F32))
    return pl.pallas_call(
        functools.partial(_combine_kernel, tm=tm, emit_x=emit_x, col_chunks=min(8, d // LANES)),
        grid_spec=pltpu.PrefetchScalarGridSpec(
            num_scalar_prefetch=2,
            grid=(m // tm,),
            in_specs=[
                pl.BlockSpec(memory_space=pl.ANY),
                row_block,
                pl.BlockSpec((tm, LANES), lambda i, p1, p2: (i, 0)),
                per_batch,
                pl.BlockSpec((1, d), lambda i, p1, p2: (0, 0)),
                per_batch,
                per_batch,
            ],
            out_specs=out_specs,
            scratch_shapes=[pltpu.VMEM((2, 2, tm * _slab_pitch(d), LANES), F32),
                            pltpu.VMEM((tm, d), F32), pltpu.SemaphoreType.DMA((2,))],
        ),
        out_shape=out_shape,
        compiler_params=_params("arbitrary"),
        name="moe_combine",
    )(pos1, pos2, y_sorted, x2d, info, gate.reshape(b, 1, d), norm_g.reshape(1, d),
      norm_shift.reshape(b, 1, d), norm_scale.reshape(b, 1, d))


EXPERT_TILE = 256


def _hier_moe(x2d, seq, layer, ffn_norm, gate, next_norm, w_rg, b_rg, w_re, b_re, w_eg, w_eu, w_ed,
              norm_dtype, emit_x):
    m, d = x2d.shape
    n_layers, n_groups, e_per_group, _, f = w_eg.shape
    n_exp = n_groups * e_per_group
    tm = EXPERT_TILE
    max_tiles = (2 * m) // tm + n_exp
    pitch = _slab_pitch(d)
    h_slabs, info, cnt = _norm_router(x2d, *ffn_norm, seq, w_rg, b_rg, w_re, b_re, e_per_group)

    counts = cnt[0, :n_exp].astype(jnp.int32)
    tiles_per_expert = (counts + tm - 1) // tm
    tile_end = jnp.cumsum(tiles_per_expert)
    tile_start = tile_end - tiles_per_expert
    n_tiles = tile_end[-1:]
    expert = info[:, INFO_E1:INFO_E2 + 1].astype(jnp.int32)
    rank = info[:, INFO_R1:INFO_R2 + 1].astype(jnp.int32)
    pos = tile_start[expert] * tm + rank
    tile_ids = jnp.arange(max_tiles + 1, dtype=jnp.int32)
    tile_expert = jnp.sum((tile_ids[:, None] >= tile_end[None, :]).astype(jnp.int32), axis=1)
    last_expert = tile_expert[jnp.maximum(n_tiles[0] - 1, 0)]
    tile_expert = jnp.where(tile_ids < n_tiles[0], tile_expert, last_expert) + layer * n_exp
    token_row = jnp.broadcast_to(jnp.arange(m, dtype=jnp.int32)[:, None] * pitch, (m, 2))
    sorted_row = jnp.zeros(((max_tiles + 1) * tm,), jnp.int32).at[pos.reshape(-1)].set(
        token_row.reshape(-1))

    y_slabs = _experts(h_slabs, tile_expert, n_tiles, sorted_row,
                       w_eg.reshape(n_layers * n_exp, d, f), w_eu.reshape(n_layers * n_exp, d, f),
                       w_ed.reshape(n_layers * n_exp, f, d), tm, max_tiles)
    return _moe_combine(y_slabs, pos[:, 0] * pitch, pos[:, 1] * pitch, x2d, info, gate, seq,
                        *next_norm, norm_dtype, emit_x)


def kernel(x, c, ada_w, ada_b, norm_mix_g, norm_ffn_g, w_in, b_forget, rel_bias, w_branch_a, w_branch_b, w_gate, b_gate, w_out, w_router_group, b_router_group, w_router_expert, b_router_expert, w_exp_gate, w_exp_up, w_exp_down, final_norm_g):
    b, s, d = x.shape
    n_layers = ada_w.shape[0]
    m = b * s
    heads_a, heads_b = b_forget.shape[1], rel_bias.shape[1]
    wa3 = 3 * heads_a * HEAD_DIM
    assert w_branch_a.shape[1] == heads_a * HEAD_DIM and w_branch_b.shape[1] == heads_b * HEAD_DIM
    assert w_in.shape[2] == wa3 + heads_a + 3 * heads_b * HEAD_DIM

    mod = _adaln_mod(c, ada_w, ada_b)
    mods = [[mod[l, :, i * d:(i + 1) * d] for i in range(6)] for l in range(n_layers)]
    w_in_t = jnp.swapaxes(w_in, 1, 2)
    zeros = jnp.zeros((b, d), F32)
    x2d = x.reshape(m, d)
    h2d = _norm_mod(x, norm_mix_g[0], mods[0][0], mods[0][1], BF16).reshape(m, d)
    for l in range(n_layers):
        _, _, gt_m, sh_f, sc_f, gt_f = mods[l]
        qkv = _in_projection(h2d, w_in_t, l, wa3, heads_a).reshape(b, s, -1)
        cum = _forget_cumsum(h2d.reshape(b, s, d), w_in_t, l, b_forget[l], wa3)
        ya = _fox_attention(qkv, cum, heads_a, (0, heads_a, 2 * heads_a))
        off = 3 * heads_a
        yb = _chunk_attention(qkv, rel_bias[l], (off, off + heads_b, off + 2 * heads_b))
        merged = _gated_merge(h2d, ya.reshape(m, -1), yb.reshape(m, -1), l, w_gate, b_gate[l],
                              w_branch_a, w_branch_b)
        x2d = _out_projection(merged, w_out, l, x2d, gt_m, s)
        last = l + 1 == n_layers
        if last:
            next_norm, norm_dtype = (final_norm_g, zeros, zeros), x.dtype
        else:
            next_norm, norm_dtype = (norm_mix_g[l + 1], mods[l + 1][0], mods[l + 1][1]), BF16
        outs = _hier_moe(x2d, s, l, (norm_ffn_g[l], sh_f, sc_f), gt_f, next_norm,
                         w_router_group[l], b_router_group[l], w_router_expert[l],
                         b_router_expert[l], w_exp_gate, w_exp_up, w_exp_down,
                         norm_dtype, emit_x=not last)
        if not last:
            x2d, h2d = outs
    return outs[0].reshape(b, s, d)
```

```python
import functools

import jax
import jax.numpy as jnp
from jax import lax
from jax.experimental import pallas as pl
from jax.experimental.pallas import tpu as pltpu

CHUNK = 64
LEFT_CHUNKS = 8
EPS = 1e-6
NEG_INF = -1e30
LANES = 128
HEAD_DIM = 128
LOG2E = 1.4426950408889634
V7X_VMEM_LIMIT_BYTES = 60000 * 1024

F32 = jnp.float32
BF16 = jnp.bfloat16


def _params(*sem):
    return pltpu.CompilerParams(dimension_semantics=sem, vmem_limit_bytes=V7X_VMEM_LIMIT_BYTES)


def _tile(n, pref, mult=LANES):
    t = (min(pref, n) // mult) * mult
    while t >= mult:
        if n % t == 0:
            return t
        t -= mult
    return n


def _cast_rows(src_ref, dst_ref, rows_per_step):
    n = src_ref.shape[0]

    def body(r, carry):
        rows = pl.ds(pl.multiple_of(r * rows_per_step, rows_per_step), rows_per_step)
        dst_ref[rows, :] = src_ref[rows, :].astype(dst_ref.dtype)
        return carry

    lax.fori_loop(0, n // rows_per_step, body, 0)


def _mod_kernel(c_ref, w_ref, b_ref, o_ref):
    c = c_ref[...]
    c_act = (c * jax.nn.sigmoid(c)).astype(BF16)
    o_ref[0] = jnp.dot(c_act, w_ref[0].astype(BF16), preferred_element_type=F32) + b_ref[0]


def _adaln_mod(c, ada_w, ada_b):
    n_layers, d, n = ada_w.shape
    b = c.shape[0]
    bp = -(-b // 8) * 8
    c_pad = jnp.pad(c, ((0, bp - b), (0, 0)))
    tn = _tile(n, 512)
    out = pl.pallas_call(
        _mod_kernel,
        grid=(n_layers, n // tn),
        in_specs=[
            pl.BlockSpec((bp, d), lambda l, j: (0, 0)),
            pl.BlockSpec((1, d, tn), lambda l, j: (l, 0, j)),
            pl.BlockSpec((1, 1, tn), lambda l, j: (l, 0, j)),
        ],
        out_specs=pl.BlockSpec((1, bp, tn), lambda l, j: (l, 0, j)),
        out_shape=jax.ShapeDtypeStruct((n_layers, bp, n), F32),
        compiler_params=_params("arbitrary", "arbitrary"),
        name="adaln_mod",
    )(c_pad, ada_w, ada_b.reshape(n_layers, 1, n))
    return out[:, :b]


def _norm_mod_kernel(x_ref, g_ref, sh_ref, sc_ref, o_ref):
    x = x_ref[0]
    ms = jnp.mean(x * x, axis=-1, keepdims=True)
    y = x * lax.rsqrt(ms + EPS) * g_ref[...]
    o_ref[0] = (y * (1.0 + sc_ref[0]) + sh_ref[0]).astype(o_ref.dtype)


def _norm_mod(x, g, shift, scale, out_dtype):
    b, s, d = x.shape
    ts = _tile(s, 256, 8)
    return pl.pallas_call(
        _norm_mod_kernel,
        grid=(b, s // ts),
        in_specs=[
            pl.BlockSpec((1, ts, d), lambda i, j: (i, j, 0)),
            pl.BlockSpec((1, d), lambda i, j: (0, 0)),
            pl.BlockSpec((1, 1, d), lambda i, j: (i, 0, 0)),
            pl.BlockSpec((1, 1, d), lambda i, j: (i, 0, 0)),
        ],
        out_specs=pl.BlockSpec((1, ts, d), lambda i, j: (i, j, 0)),
        out_shape=jax.ShapeDtypeStruct((b, s, d), out_dtype),
        compiler_params=_params("arbitrary", "arbitrary"),
        name="norm_mod",
    )(x, g.reshape(1, d), shift.reshape(b, 1, d), scale.reshape(b, 1, d))


def _dot_nt(a, b):
    return lax.dot_general(a, b, (((1,), (1,)), ((), ())), preferred_element_type=F32)


def _inproj_kernel(h_ref, w_ref, o_ref, wbf_ref, *, rows_per_step):
    @pl.when(pl.program_id(1) == 0)
    def _():
        _cast_rows(w_ref, wbf_ref, rows_per_step)

    o_ref[...] = _dot_nt(h_ref[...], wbf_ref[...]).astype(o_ref.dtype)


def _in_projection(h2d, w_in_t, layer, w_a3, n_forget):
    m, d = h2d.shape
    n_out = w_in_t.shape[1] - n_forget
    tn = _tile(w_a3, 512)
    assert n_out % tn == 0 and w_a3 % tn == 0 and n_forget % 8 == 0
    tm = _tile(m, 1024, 8)
    n_plain = w_a3 // tn
    kern = functools.partial(_inproj_kernel, rows_per_step=_tile(tn, 128, 8))
    return pl.pallas_call(
        kern,
        grid=(n_out // tn, m // tm),
        in_specs=[
            pl.BlockSpec((tm, d), lambda j, i: (i, 0)),
            pl.BlockSpec((None, pl.Element(tn), pl.Element(d)),
                         lambda j, i: (layer, pl.multiple_of(
                             j * tn + jnp.where(j >= n_plain, n_forget, 0), 8), 0)),
        ],
        out_specs=pl.BlockSpec((tm, tn), lambda j, i: (i, j)),
        out_shape=jax.ShapeDtypeStruct((m, n_out), BF16),
        scratch_shapes=[pltpu.VMEM((tn, d), BF16)],
        compiler_params=_params("arbitrary", "arbitrary"),
        name="in_projection",
    )(h2d, w_in_t)


def _split3_dot(x, u):
    x1 = x.astype(BF16)
    r1 = x - x1.astype(F32)
    x2 = r1.astype(BF16)
    x3 = (r1 - x2.astype(F32)).astype(BF16)
    ub = u.astype(BF16)
    return (jnp.dot(x1, ub, preferred_element_type=F32)
            + jnp.dot(x2, ub, preferred_element_type=F32)
            + jnp.dot(x3, ub, preferred_element_type=F32))


def _forget_kernel(h_ref, w_ref, b_ref, o_ref, carry_ref):
    s = pl.program_id(1)

    @pl.when(s == 0)
    def _():
        carry_ref[...] = jnp.zeros_like(carry_ref)

    ts = h_ref.shape[1]
    f = _dot_nt(w_ref[...].astype(BF16), h_ref[0]) + b_ref[...]
    log_f = jnp.minimum(f, 0.0) - jnp.log1p(jnp.exp(-jnp.abs(f)))
    row = lax.broadcasted_iota(jnp.int32, (ts, ts), 0)
    col = lax.broadcasted_iota(jnp.int32, (ts, ts), 1)
    upper = jnp.where(row <= col, 1.0, 0.0).astype(F32)
    cs = _split3_dot(log_f, upper) + carry_ref[:, 0:1]
    o_ref[0] = cs * LOG2E
    carry_ref[...] = jnp.broadcast_to(cs[:, ts - 1:ts], carry_ref.shape)


def _forget_cumsum(h, w_in_t, layer, b_forget, row0):
    b, s, d = h.shape
    n_heads = b_forget.shape[0]
    hp = -(-n_heads // 8) * 8
    assert row0 % 8 == 0
    ts = _tile(s, 512)
    bias = jnp.pad(b_forget.astype(F32), (0, hp - n_heads)).reshape(hp, 1)
    return pl.pallas_call(
        _forget_kernel,
        grid=(b, s // ts),
        in_specs=[
            pl.BlockSpec((1, ts, d), lambda i, j: (i, j, 0)),
            pl.BlockSpec((None, pl.Element(hp), pl.Element(d)), lambda i, j: (layer, row0, 0)),
            pl.BlockSpec((hp, 1), lambda i, j: (0, 0)),
        ],
        out_specs=pl.BlockSpec((1, hp, ts), lambda i, j: (i, 0, j)),
        out_shape=jax.ShapeDtypeStruct((b, hp, s), F32),
        scratch_shapes=[pltpu.VMEM((hp, LANES), F32)],
        compiler_params=_params("arbitrary", "arbitrary"),
        name="forget_cumsum",
    )(h, w_in_t, bias)


def _fox_kernel(q_ref, k_ref, v_ref, cum_ref, o_ref, *, tq, scale_log2e):
    head = pl.program_id(1)
    seq = q_ref.shape[1]
    row = lax.broadcasted_iota(jnp.int32, (tq, tq), 0)
    col = lax.broadcasted_iota(jnp.int32, (tq, tq), 1)
    causal = row >= col
    decay = cum_ref[0, pl.ds(head, 1), :]

    n_blocks = seq // tq

    def scores(iq):
        q0 = iq * tq
        q = q_ref[0, q0:q0 + tq, :]
        s_diag = _dot_nt(q, k_ref[0, q0:q0 + tq, :]) * scale_log2e - decay[:, q0:q0 + tq]
        s_diag = jnp.where(causal, s_diag, NEG_INF)
        s_past = None
        if iq > 0:
            s_past = _dot_nt(q, k_ref[0, 0:q0, :]) * scale_log2e - decay[:, 0:q0]
        return s_diag, s_past

    nxt = scores(0)
    for iq in range(n_blocks):
        q0 = iq * tq
        s_diag, s_past = nxt
        if iq + 1 < n_blocks:
            nxt = scores(iq + 1)
        m = jnp.max(s_diag, axis=-1, keepdims=True)
        if iq > 0:
            m = jnp.maximum(m, jnp.max(s_past, axis=-1, keepdims=True))
        p_diag = jnp.exp2(s_diag - m)
        l = jnp.sum(p_diag, axis=-1, keepdims=True)
        acc = jnp.dot(p_diag.astype(BF16), v_ref[0, q0:q0 + tq, :], preferred_element_type=F32)
        if iq > 0:
            p_past = jnp.exp2(s_past - m)
            l = l + jnp.sum(p_past, axis=-1, keepdims=True)
            acc = acc + jnp.dot(p_past.astype(BF16), v_ref[0, 0:q0, :], preferred_element_type=F32)
        o_ref[0, q0:q0 + tq, :] = (acc / l).astype(o_ref.dtype)


def _fox_attention(qkv, cum, n_heads, col_blocks):
    b, s, _ = qkv.shape
    hp = cum.shape[1]
    tq = _tile(s, 256)
    qc, kc, vc = col_blocks
    kern = functools.partial(_fox_kernel, tq=tq, scale_log2e=HEAD_DIM ** -0.5 * LOG2E)
    return pl.pallas_call(
        kern,
        grid=(b, n_heads),
        in_specs=[
            pl.BlockSpec((1, s, HEAD_DIM), lambda i, h: (i, 0, qc + h)),
            pl.BlockSpec((1, s, HEAD_DIM), lambda i, h: (i, 0, kc + h)),
            pl.BlockSpec((1, s, HEAD_DIM), lambda i, h: (i, 0, vc + h)),
            pl.BlockSpec((1, hp, s), lambda i, h: (i, 0, 0)),
        ],
        out_specs=pl.BlockSpec((1, s, HEAD_DIM), lambda i, h: (i, 0, h)),
        out_shape=jax.ShapeDtypeStruct((b, s, n_heads * HEAD_DIM), BF16),
        compiler_params=_params("arbitrary", "arbitrary"),
        name="fox_attention",
    )(qkv, qkv, qkv, cum)


Q_CHUNKS = 2
QB = Q_CHUNKS * CHUNK
PAD = LEFT_CHUNKS * CHUNK
WIN = (LEFT_CHUNKS + Q_CHUNKS) * CHUNK
CHUNK_SHIFT = CHUNK.bit_length() - 1


def _chunk_bias_kernel(tab_ref, o_ref, *, max_rel, c0):
    h = pl.program_id(0)

    def visible(qi, kj):
        q_chunk = jnp.right_shift(qi, CHUNK_SHIFT)
        k_chunk = jnp.right_shift(kj, CHUNK_SHIFT)
        return jnp.logical_and(k_chunk >= q_chunk, k_chunk <= q_chunk + LEFT_CHUNKS)

    if c0 > 0:
        qi = lax.broadcasted_iota(jnp.int32, (QB, c0), 0)
        kj = lax.broadcasted_iota(jnp.int32, (QB, c0), 1)
        far = tab_ref[h, 2 * max_rel] * LOG2E
        o_ref[0, :, 0:c0] = jnp.where(visible(qi, kj), far, NEG_INF)
    qi = lax.broadcasted_iota(jnp.int32, (QB, WIN - c0), 0)
    kj = lax.broadcasted_iota(jnp.int32, (QB, WIN - c0), 1) + c0
    idx = jnp.clip(PAD + qi - kj, -max_rel, max_rel) + max_rel

    def body(r, acc):
        return jnp.where(idx == r, tab_ref[h, r], acc)

    near = lax.fori_loop(0, 2 * max_rel + 1, body, jnp.zeros((QB, WIN - c0), F32), unroll=8)
    o_ref[0, :, c0:WIN] = jnp.where(visible(qi, kj), near * LOG2E, NEG_INF)


def _chunk_bias(rel_table):
    n_heads, n_rel = rel_table.shape
    max_rel = (n_rel - 1) // 2
    c0 = max(0, (PAD - max_rel + 1) // LANES * LANES)
    return pl.pallas_call(
        functools.partial(_chunk_bias_kernel, max_rel=max_rel, c0=c0),
        grid=(n_heads,),
        in_specs=[pl.BlockSpec(memory_space=pltpu.SMEM)],
        out_specs=pl.BlockSpec((1, QB, WIN), lambda h: (h, 0, 0)),
        out_shape=jax.ShapeDtypeStruct((n_heads, QB, WIN), F32),
        compiler_params=_params("arbitrary"),
        name="chunk_bias",
    )(rel_table.astype(F32))


def _chunk_kernel(q_ref, k_ref, v_ref, bias_ref, o_ref, *, scale_log2e):
    seq = q_ref.shape[1]
    n_blocks = seq // QB

    def scores(i):
        q0 = i * QB
        k_lo = max(0, q0 - PAD)
        width = q0 + QB - k_lo
        q = q_ref[0, q0:q0 + QB, :]
        return _dot_nt(q, k_ref[0, k_lo:q0 + QB, :]) * scale_log2e + bias_ref[0, :, WIN - width:WIN]

    s_next = scores(0)
    for i in range(n_blocks):
        q0 = i * QB
        k_lo = max(0, q0 - PAD)
        s = s_next
        if i + 1 < n_blocks:
            s_next = scores(i + 1)
        m = jnp.max(s, axis=-1, keepdims=True)
        p = jnp.exp2(s - m)
        l = jnp.sum(p, axis=-1, keepdims=True)
        out = jnp.dot(p.astype(BF16), v_ref[0, k_lo:q0 + QB, :], preferred_element_type=F32)
        o_ref[0, q0:q0 + QB, :] = (out / l).astype(o_ref.dtype)


def _chunk_attention(qkv, rel_table, col_blocks):
    b, s, _ = qkv.shape
    n_heads = rel_table.shape[0]
    assert s % QB == 0
    qc, kc, vc = col_blocks
    kern = functools.partial(_chunk_kernel, scale_log2e=HEAD_DIM ** -0.5 * LOG2E)
    return pl.pallas_call(
        kern,
        grid=(b, n_heads),
        in_specs=[
            pl.BlockSpec((1, s, HEAD_DIM), lambda i, h: (i, 0, qc + h)),
            pl.BlockSpec((1, s, HEAD_DIM), lambda i, h: (i, 0, kc + h)),
            pl.BlockSpec((1, s, HEAD_DIM), lambda i, h: (i, 0, vc + h)),
            pl.BlockSpec((1, QB, WIN), lambda i, h: (h, 0, 0)),
        ],
        out_specs=pl.BlockSpec((1, s, HEAD_DIM), lambda i, h: (i, 0, h)),
        out_shape=jax.ShapeDtypeStruct((b, s, n_heads * HEAD_DIM), BF16),
        compiler_params=_params("arbitrary", "arbitrary"),
        name="chunk_attention",
    )(qkv, qkv, qkv, _chunk_bias(rel_table))


def _merge_kernel(h_ref, ya_ref, yb_ref, wga_ref, wgb_ref, wa_ref, wb_ref, bga_ref, bgb_ref, o_ref,
                  wga_s, wgb_s, wa_s, wb_s, *, rows_per_step):
    @pl.when(pl.program_id(1) == 0)
    def _():
        _cast_rows(wga_ref, wga_s, rows_per_step)
        _cast_rows(wgb_ref, wgb_s, rows_per_step)
        _cast_rows(wa_ref, wa_s, rows_per_step)
        _cast_rows(wb_ref, wb_s, rows_per_step)

    h = h_ref[...]
    gate_a = jax.nn.sigmoid(jnp.dot(h, wga_s[...], preferred_element_type=F32) + bga_ref[...])
    gate_b = jax.nn.sigmoid(jnp.dot(h, wgb_s[...], preferred_element_type=F32) + bgb_ref[...])
    proj_a = jnp.dot(ya_ref[...], wa_s[...], preferred_element_type=F32)
    proj_b = jnp.dot(yb_ref[...], wb_s[...], preferred_element_type=F32)
    o_ref[...] = (gate_a * proj_a + gate_b * proj_b).astype(o_ref.dtype)


def _gated_merge(h2d, ya2d, yb2d, layer, w_gate, b_gate, w_a, w_b):
    m, d = h2d.shape
    wa_rows, wb_rows = w_a.shape[1], w_b.shape[1]
    tn = _tile(d, 256)
    tm = _tile(m, 1024, 8)
    nb = d // tn
    single = pl.Buffered(1)
    kern = functools.partial(_merge_kernel, rows_per_step=_tile(min(d, wa_rows, wb_rows), 256, 8))
    b2 = b_gate.reshape(1, 2 * d)
    return pl.pallas_call(
        kern,
        grid=(nb, m // tm),
        in_specs=[
            pl.BlockSpec((tm, d), lambda j, i: (i, 0)),
            pl.BlockSpec((tm, wa_rows), lambda j, i: (i, 0)),
            pl.BlockSpec((tm, wb_rows), lambda j, i: (i, 0)),
            pl.BlockSpec((None, d, tn), lambda j, i: (layer, 0, j), pipeline_mode=single),
            pl.BlockSpec((None, d, tn), lambda j, i: (layer, 0, nb + j), pipeline_mode=single),
            pl.BlockSpec((None, wa_rows, tn), lambda j, i: (layer, 0, j), pipeline_mode=single),
            pl.BlockSpec((None, wb_rows, tn), lambda j, i: (layer, 0, j), pipeline_mode=single),
            pl.BlockSpec((1, tn), lambda j, i: (0, j)),
            pl.BlockSpec((1, tn), lambda j, i: (0, nb + j)),
        ],
        out_specs=pl.BlockSpec((tm, tn), lambda j, i: (i, j)),
        out_shape=jax.ShapeDtypeStruct((m, d), BF16),
        scratch_shapes=[pltpu.VMEM((d, tn), BF16), pltpu.VMEM((d, tn), BF16),
                        pltpu.VMEM((wa_rows, tn), BF16), pltpu.VMEM((wb_rows, tn), BF16)],
        compiler_params=_params("arbitrary", "arbitrary"),
        name="gated_merge",
    )(h2d, ya2d, yb2d, w_gate, w_gate, w_a, w_b, b2, b2)


def _outproj_kernel(a_ref, w_ref, x_ref, gt_ref, o_ref, w_s, *, rows_per_step):
    @pl.when(pl.program_id(1) == 0)
    def _():
        _cast_rows(w_ref, w_s, rows_per_step)

    y = jnp.dot(a_ref[...], w_s[...], preferred_element_type=F32)
    o_ref[...] = x_ref[...] + gt_ref[0] * y


def _out_projection(a2d, w_out, layer, x2d, gate, seq):
    m, k = a2d.shape
    d = w_out.shape[2]
    b = gate.shape[0]
    tn = _tile(d, 512)
    tm = _tile(seq, 1024, 8)
    steps_per_batch = seq // tm
    kern = functools.partial(_outproj_kernel, rows_per_step=_tile(k, 256, 8))
    return pl.pallas_call(
        kern,
        grid=(d // tn, m // tm),
        in_specs=[
            pl.BlockSpec((tm, k), lambda j, i: (i, 0)),
            pl.BlockSpec((None, k, tn), lambda j, i: (layer, 0, j), pipeline_mode=pl.Buffered(1)),
            pl.BlockSpec((tm, tn), lambda j, i: (i, j)),
            pl.BlockSpec((1, 1, tn), lambda j, i: (i // steps_per_batch, 0, j)),
        ],
        out_specs=pl.BlockSpec((tm, tn), lambda j, i: (i, j)),
        out_shape=jax.ShapeDtypeStruct((m, d), F32),
        scratch_shapes=[pltpu.VMEM((k, tn), BF16)],
        compiler_params=_params("arbitrary", "arbitrary"),
        name="out_projection",
    )(a2d, w_out, x2d, gate.reshape(b, 1, d))


INFO_E1, INFO_E2, INFO_W1, INFO_W2, INFO_R1, INFO_R2 = range(6)


def _router_kernel(x_ref, g_ref, sh_ref, sc_ref, w_ref, b_ref, h_ref, info_ref, cnt_ref, carry_ref,
                   *, n_groups, e_per_group):
    @pl.when(pl.program_id(0) == 0)
    def _():
        carry_ref[...] = jnp.zeros_like(carry_ref)

    tm = x_ref.shape[0]
    x = x_ref[...]
    ms = jnp.mean(x * x, axis=-1, keepdims=True)
    h = x * lax.rsqrt(ms + EPS) * g_ref[...] * (1.0 + sc_ref[0]) + sh_ref[0]
    _store_slabs(h_ref, h)
    logits = jnp.dot(h.astype(BF16), w_ref[...].astype(BF16),
                     preferred_element_type=F32) + b_ref[...]
    lane = lax.broadcasted_iota(jnp.int32, (tm, LANES), 1).astype(F32)
    big = float(LANES)
    is_group = lane < n_groups
    gl = jnp.where(is_group, logits, -jnp.inf)
    g_max = jnp.max(gl, axis=-1, keepdims=True)
    g_sel = jnp.min(jnp.where(gl == g_max, lane, big), axis=-1, keepdims=True)
    g_w = 1.0 / jnp.sum(jnp.where(is_group, jnp.exp(gl - g_max), 0.0), axis=-1, keepdims=True)

    expert = lane - n_groups
    lo = g_sel * e_per_group
    in_group = jnp.logical_and(expert >= lo, expert < lo + e_per_group)
    el = jnp.where(in_group, logits, -jnp.inf)
    v1 = jnp.max(el, axis=-1, keepdims=True)
    e1 = jnp.min(jnp.where(el == v1, expert, big), axis=-1, keepdims=True)
    el2 = jnp.where(expert == e1, -jnp.inf, el)
    v2 = jnp.max(el2, axis=-1, keepdims=True)
    e2 = jnp.min(jnp.where(el2 == v2, expert, big), axis=-1, keepdims=True)
    t = jnp.exp(v2 - v1)
    w1 = g_w / (1.0 + t)
    w2 = g_w * t / (1.0 + t)

    onehot = jnp.where(jnp.logical_or(lane == e1, lane == e2), 1.0, 0.0)
    row = lax.broadcasted_iota(jnp.int32, (tm, tm), 0)
    col = lax.broadcasted_iota(jnp.int32, (tm, tm), 1)
    before = jnp.where(col < row, 1.0, 0.0).astype(BF16)
    prior = jnp.dot(before, onehot.astype(BF16), preferred_element_type=F32) + carry_ref[0:1, :]
    r1 = jnp.sum(jnp.where(lane == e1, prior, 0.0), axis=-1, keepdims=True)
    r2 = jnp.sum(jnp.where(lane == e2, prior, 0.0), axis=-1, keepdims=True)
    total = carry_ref[0:1, :] + jnp.sum(onehot, axis=0, keepdims=True)
    carry_ref[...] = jnp.broadcast_to(total, carry_ref.shape)
    cnt_ref[...] = jnp.broadcast_to(total, cnt_ref.shape)

    info = jnp.zeros((tm, LANES), F32)
    for idx, val in ((INFO_E1, e1), (INFO_E2, e2), (INFO_W1, w1), (INFO_W2, w2),
                     (INFO_R1, r1), (INFO_R2, r2)):
        info = jnp.where(lane == idx, val, info)
    info_ref[...] = info


def _norm_router(x2d, g, shift, scale, seq, w_rg, b_rg, w_re, b_re, e_per_group):
    m, d = x2d.shape
    pitch = _slab_pitch(d)
    b = shift.shape[0]
    n_groups = w_rg.shape[1]
    n_exp = w_re.shape[1]
    assert n_groups + n_exp <= LANES
    pad = LANES - n_groups - n_exp
    w = jnp.pad(jnp.concatenate([w_rg, w_re], axis=1), ((0, 0), (0, pad)))
    bias = jnp.pad(jnp.concatenate([b_rg, b_re]).astype(F32), (0, pad)).reshape(1, LANES)
    tm = _tile(seq, 256, 8)
    steps_per_batch = seq // tm
    kern = functools.partial(_router_kernel, n_groups=n_groups, e_per_group=e_per_group)
    return pl.pallas_call(
        kern,
        grid=(m // tm,),
        in_specs=[
            pl.BlockSpec((tm, d), lambda i: (i, 0)),
            pl.BlockSpec((1, d), lambda i: (0, 0)),
            pl.BlockSpec((1, 1, d), lambda i: (i // steps_per_batch, 0, 0)),
            pl.BlockSpec((1, 1, d), lambda i: (i // steps_per_batch, 0, 0)),
            pl.BlockSpec((d, LANES), lambda i: (0, 0)),
            pl.BlockSpec((1, LANES), lambda i: (0, 0)),
        ],
        out_specs=[pl.BlockSpec((tm * pitch, LANES), lambda i: (i, 0)),
                   pl.BlockSpec((tm, LANES), lambda i: (i, 0)),
                   pl.BlockSpec((8, LANES), lambda i: (0, 0))],
        out_shape=[jax.ShapeDtypeStruct((m * pitch, LANES), F32),
                   jax.ShapeDtypeStruct((m, LANES), F32), jax.ShapeDtypeStruct((8, LANES), F32)],
        scratch_shapes=[pltpu.VMEM((8, LANES), F32)],
        compiler_params=_params("arbitrary"),
        name="moe_norm_router",
    )(x2d, g.reshape(1, d), shift.reshape(b, 1, d), scale.reshape(b, 1, d), w, bias)


def _slab_rows(d):
    return d // LANES


def _slab_pitch(d):
    pitch = -(-_slab_rows(d) // 8) * 8
    return pitch if (pitch // 8) % 2 else pitch + 8


def _store_slabs(ref, val):
    n, d = val.shape
    pitch = ref.shape[0] // n
    for s in range(pitch):
        if s < _slab_rows(d):
            piece = val[:, s * LANES:(s + 1) * LANES]
        else:
            piece = jnp.zeros((n, LANES), val.dtype)
        ref[pl.ds(s, n, stride=pitch), :] = piece


def _load_slabs(ref, n, pitch, s0, s1):
    return jnp.concatenate([ref[pl.ds(s, n, stride=pitch), :] for s in range(s0, s1)], axis=1)


def _slab_copy(src_hbm, src_row, dst_ref, dst_row, rows, sem):
    return pltpu.make_async_copy(src_hbm.at[pl.ds(pl.multiple_of(src_row, 8), rows)],
                                 dst_ref.at[pl.ds(pl.multiple_of(dst_row, 8), rows)], sem)


def _gather_slabs(idx_ref, base, n, src_hbm, dst_ref, pitch, rows, sem):
    def body(r, carry):
        _slab_copy(src_hbm, idx_ref[base + r], dst_ref, r * pitch, rows, sem).start()
        return carry

    lax.fori_loop(0, n, body, 0, unroll=8)


def _wait_slabs(n, src_hbm, dst_ref, rows, sem):
    pltpu.make_async_copy(src_hbm.at[pl.ds(0, n * rows)], dst_ref.at[pl.ds(0, n * rows)], sem).wait()


def _expert_changed(te_ref, i):
    return jnp.logical_or(i == 0, te_ref[i] != te_ref[jnp.maximum(i - 1, 0)])


def _expert_weights_step(i, n_tiles, te_ref, nxt_ref, run_ref, w_hbm, wbuf, w_s, sem, rows_per_step):
    def copies(expert, slot):
        return [pltpu.make_async_copy(w.at[expert], wbuf.at[slot, k], sem.at[slot])
                for k, w in enumerate(w_hbm)]

    @pl.when(i == 0)
    def _():
        run_ref[0] = 0
        for cp in copies(te_ref[0], 0):
            cp.start()

    @pl.when(jnp.logical_and(i < n_tiles, _expert_changed(te_ref, i)))
    def _():
        slot = run_ref[0] % 2
        for cp in copies(te_ref[i], slot):
            cp.wait()

        @pl.when(nxt_ref[i] != te_ref[i])
        def _():
            for cp in copies(nxt_ref[i], 1 - slot):
                cp.start()

        for k, dst in enumerate(w_s):
            _cast_rows(wbuf.at[slot, k], dst, rows_per_step)
        run_ref[0] = run_ref[0] + 1


def _expert_up_kernel(te_ref, nt_ref, nxt_ref, tok_ref, h_hbm, wg_hbm, wu_hbm, a_ref, xbuf, wbuf,
                      wg_s, wu_s, sem, sem_w, run_ref, *, tm, rows_per_step, k_chunks):
    i = pl.program_id(0)
    n_tiles = nt_ref[0]
    d = wg_s.shape[0]
    rows = _slab_rows(d)
    pitch = xbuf.shape[1] // tm
    kc = d // k_chunks
    issue_chunks = max(1, k_chunks // 2)
    rows_per_chunk = tm // issue_chunks

    _expert_weights_step(i, n_tiles, te_ref, nxt_ref, run_ref, (wg_hbm, wu_hbm), wbuf,
                         (wg_s, wu_s), sem_w, rows_per_step)

    @pl.when(i == 0)
    def _():
        _gather_slabs(tok_ref, 0, tm, h_hbm, xbuf.at[0], pitch, rows, sem.at[0])

    @pl.when(i < n_tiles)
    def _():
        slot = i % 2
        nxt = 1 - slot
        _wait_slabs(tm, h_hbm, xbuf.at[slot], rows, sem.at[slot])
        g = u = None
        for c in range(k_chunks):
            if c < issue_chunks:
                for r in range(c * rows_per_chunk, (c + 1) * rows_per_chunk):
                    _slab_copy(h_hbm, tok_ref[(i + 1) * tm + r], xbuf.at[nxt], r * pitch, rows,
                               sem.at[nxt]).start()
            x = _load_slabs(xbuf.at[slot], tm, pitch, c * kc // LANES, (c + 1) * kc // LANES)
            x = x.astype(BF16)
            gc = jnp.dot(x, wg_s[c * kc:(c + 1) * kc, :], preferred_element_type=F32)
            uc = jnp.dot(x, wu_s[c * kc:(c + 1) * kc, :], preferred_element_type=F32)
            g = gc if g is None else g + gc
            u = uc if u is None else u + uc
        a_ref[...] = (g * jax.nn.sigmoid(g) * u).astype(a_ref.dtype)

    @pl.when(i == n_tiles)
    def _():
        _wait_slabs(tm, h_hbm, xbuf.at[i % 2], rows, sem.at[i % 2])

    @pl.when(i >= n_tiles)
    def _():
        a_ref[...] = jnp.zeros_like(a_ref)


def _expert_down_kernel(te_ref, nt_ref, nxt_ref, a_ref, wd_hbm, y_ref, wbuf, wd_s, sem_w, run_ref,
                        *, rows_per_step):
    i = pl.program_id(0)
    _expert_weights_step(i, nt_ref[0], te_ref, nxt_ref, run_ref, (wd_hbm,), wbuf, (wd_s,), sem_w,
                         rows_per_step)

    @pl.when(i < nt_ref[0])
    def _():
        _store_slabs(y_ref, jnp.dot(a_ref[...], wd_s[...], preferred_element_type=F32))

    @pl.when(i >= nt_ref[0])
    def _():
        y_ref[...] = jnp.zeros_like(y_ref)


def _experts(h_slabs, tile_expert, n_tiles, next_expert, sorted_row, w_gate, w_up, w_down, tm,
             max_tiles):
    n_exp, d, f = w_gate.shape
    pitch = _slab_pitch(d)
    k_chunks = min(8, d // LANES)
    assert d % (k_chunks * LANES) == 0 and tm % k_chunks == 0
    a = pl.pallas_call(
        functools.partial(_expert_up_kernel, tm=tm, rows_per_step=_tile(d, 256, 8),
                          k_chunks=k_chunks),
        grid_spec=pltpu.PrefetchScalarGridSpec(
            num_scalar_prefetch=4,
            grid=(max_tiles + 1,),
            in_specs=[
                pl.BlockSpec(memory_space=pl.ANY),
                pl.BlockSpec(memory_space=pl.ANY),
                pl.BlockSpec(memory_space=pl.ANY),
            ],
            out_specs=pl.BlockSpec((tm, f), lambda i, te, nt, nxt, tok: (i, 0)),
            scratch_shapes=[pltpu.VMEM((2, tm * pitch, LANES), F32),
                            pltpu.VMEM((2, 2, d, f), F32),
                            pltpu.VMEM((d, f), BF16), pltpu.VMEM((d, f), BF16),
                            pltpu.SemaphoreType.DMA((2,)), pltpu.SemaphoreType.DMA((2,)),
                            pltpu.SMEM((1,), jnp.int32)],
        ),
        out_shape=jax.ShapeDtypeStruct(((max_tiles + 1) * tm, f), BF16),
        compiler_params=_params("arbitrary"),
        name="moe_expert_up",
    )(tile_expert, n_tiles, next_expert, sorted_row, h_slabs, w_gate, w_up)
    return pl.pallas_call(
        functools.partial(_expert_down_kernel, rows_per_step=_tile(f, 64, 8)),
        grid_spec=pltpu.PrefetchScalarGridSpec(
            num_scalar_prefetch=3,
            grid=(max_tiles,),
            in_specs=[
                pl.BlockSpec((tm, f), lambda i, te, nt, nxt: (i, 0)),
                pl.BlockSpec(memory_space=pl.ANY),
            ],
            out_specs=pl.BlockSpec((tm * pitch, LANES), lambda i, te, nt, nxt: (i, 0)),
            scratch_shapes=[pltpu.VMEM((2, 1, f, d), F32), pltpu.VMEM((f, d), BF16),
                            pltpu.SemaphoreType.DMA((2,)), pltpu.SMEM((1,), jnp.int32)],
        ),
        out_shape=jax.ShapeDtypeStruct((max_tiles * tm * pitch, LANES), F32),
        compiler_params=_params("arbitrary"),
        name="moe_expert_down",
    )(tile_expert, n_tiles, next_expert, a, w_down)


def _combine_kernel(p1_ref, p2_ref, y_hbm, x_ref, info_ref, gt_ref, g_ref, sh_ref, sc_ref, *refs,
                    tm, emit_x, col_chunks):
    if emit_x:
        x_out, h_out, ybuf, xs, sem = refs
    else:
        h_out, ybuf, xs, sem = refs
    i = pl.program_id(0)
    n_steps = pl.num_programs(0)
    d = x_ref.shape[1]
    rows = _slab_rows(d)
    pitch = ybuf.shape[2] // tm
    cw = d // col_chunks
    issue_chunks = max(1, col_chunks // 2)
    rows_per_chunk = tm // issue_chunks

    @pl.when(i == 0)
    def _():
        _gather_slabs(p1_ref, 0, tm, y_hbm, ybuf.at[0, 0], pitch, rows, sem.at[0])
        _gather_slabs(p2_ref, 0, tm, y_hbm, ybuf.at[0, 1], pitch, rows, sem.at[0])

    slot = i % 2
    nxt = 1 - slot
    nxt_base = jnp.minimum(i + 1, n_steps - 1) * tm
    _wait_slabs(tm, y_hbm, ybuf.at[slot, 0], rows, sem.at[slot])
    _wait_slabs(tm, y_hbm, ybuf.at[slot, 1], rows, sem.at[slot])
    info = info_ref[...]
    w1 = info[:, INFO_W1:INFO_W1 + 1]
    w2 = info[:, INFO_W2:INFO_W2 + 1]
    ssq = jnp.zeros((tm, 1), F32)
    for c in range(col_chunks):
        if c < issue_chunks:
            for r in range(c * rows_per_chunk, (c + 1) * rows_per_chunk):
                for k, p_ref in enumerate((p1_ref, p2_ref)):
                    _slab_copy(y_hbm, p_ref[nxt_base + r], ybuf.at[nxt, k], r * pitch, rows,
                               sem.at[nxt]).start()
        cols = slice(c * cw, (c + 1) * cw)
        s0, s1 = c * cw // LANES, (c + 1) * cw // LANES
        y = (w1 * _load_slabs(ybuf.at[slot, 0], tm, pitch, s0, s1)
             + w2 * _load_slabs(ybuf.at[slot, 1], tm, pitch, s0, s1))
        xc = x_ref[:, cols] + gt_ref[0][:, cols] * y
        xs[:, cols] = xc
        ssq = ssq + jnp.sum(xc * xc, axis=-1, keepdims=True)
    x = xs[...]
    if emit_x:
        x_out[...] = x
    h = x * lax.rsqrt(ssq * (1.0 / d) + EPS) * g_ref[...] * (1.0 + sc_ref[0]) + sh_ref[0]
    h_out[...] = h.astype(h_out.dtype)

    @pl.when(i == n_steps - 1)
    def _():
        _wait_slabs(tm, y_hbm, ybuf.at[nxt, 0], rows, sem.at[nxt])
        _wait_slabs(tm, y_hbm, ybuf.at[nxt, 1], rows, sem.at[nxt])


def _moe_combine(y_sorted, pos1, pos2, x2d, info, gate, seq, norm_g, norm_shift, norm_scale,
                 norm_dtype, emit_x):
    m, d = x2d.shape
    b = gate.shape[0]
    tm = _tile(seq, 128, 8)
    steps_per_batch = seq // tm
    row_block = pl.BlockSpec((tm, d), lambda i, p1, p2: (i, 0))
    per_batch = pl.BlockSpec((1, 1, d), lambda i, p1, p2: (i // steps_per_batch, 0, 0))
    out_specs = [row_block, row_block] if emit_x else [row_block]
    out_shape = [jax.ShapeDtypeStruct((m, d), norm_dtype)]
    if emit_x:
        out_shape.insert(0, jax.ShapeDtypeStruct((m, d), F32))
    return pl.pallas_call(
        functools.partial(_combine_kernel, tm=tm, emit_x=emit_x, col_chunks=min(8, d // LANES)),
        grid_spec=pltpu.PrefetchScalarGridSpec(
            num_scalar_prefetch=2,
            grid=(m // tm,),
            in_specs=[
                pl.BlockSpec(memory_space=pl.ANY),
                row_block,
                pl.BlockSpec((tm, LANES), lambda i, p1, p2: (i, 0)),
                per_batch,
                pl.BlockSpec((1, d), lambda i, p1, p2: (0, 0)),
                per_batch,
                per_batch,
            ],
            out_specs=out_specs,
            scratch_shapes=[pltpu.VMEM((2, 2, tm * _slab_pitch(d), LANES), F32),
                            pltpu.VMEM((tm, d), F32), pltpu.SemaphoreType.DMA((2,))],
        ),
        out_shape=out_shape,
        compiler_params=_params("arbitrary"),
        name="moe_combine",
    )(pos1, pos2, y_sorted, x2d, info, gate.reshape(b, 1, d), norm_g.reshape(1, d),
      norm_shift.reshape(b, 1, d), norm_scale.reshape(b, 1, d))


EXPERT_TILE = 256


def _hier_moe(x2d, seq, layer, ffn_norm, gate, next_norm, w_rg, b_rg, w_re, b_re, w_eg, w_eu, w_ed,
              norm_dtype, emit_x):
    m, d = x2d.shape
    n_layers, n_groups, e_per_group, _, f = w_eg.shape
    n_exp = n_groups * e_per_group
    tm = EXPERT_TILE
    max_tiles = (2 * m) // tm + n_exp
    pitch = _slab_pitch(d)
    h_slabs, info, cnt = _norm_router(x2d, *ffn_norm, seq, w_rg, b_rg, w_re, b_re, e_per_group)

    counts = cnt[0, :n_exp].astype(jnp.int32)
    tiles_per_expert = (counts + tm - 1) // tm
    tile_end = jnp.cumsum(tiles_per_expert)
    tile_start = tile_end - tiles_per_expert
    n_tiles = tile_end[-1:]
    expert = info[:, INFO_E1:INFO_E2 + 1].astype(jnp.int32)
    rank = info[:, INFO_R1:INFO_R2 + 1].astype(jnp.int32)
    pos = tile_start[expert] * tm + rank
    tile_ids = jnp.arange(max_tiles + 1, dtype=jnp.int32)
    tile_expert = jnp.sum((tile_ids[:, None] >= tile_end[None, :]).astype(jnp.int32), axis=1)
    last_expert = tile_expert[jnp.maximum(n_tiles[0] - 1, 0)]
    tile_expert = jnp.where(tile_ids < n_tiles[0], tile_expert, last_expert)
    experts = jnp.arange(n_exp, dtype=jnp.int32)
    later = jnp.logical_and(experts[None, :] > experts[:, None], tiles_per_expert[None, :] > 0)
    following = jnp.min(jnp.where(later, experts[None, :], n_exp), axis=1)
    following = jnp.where(following == n_exp, experts, following)
    next_expert = following[tile_expert] + layer * n_exp
    tile_expert = tile_expert + layer * n_exp
    token_row = jnp.broadcast_to(jnp.arange(m, dtype=jnp.int32)[:, None] * pitch, (m, 2))
    sorted_row = jnp.zeros(((max_tiles + 1) * tm,), jnp.int32).at[pos.reshape(-1)].set(
        token_row.reshape(-1))

    y_slabs = _experts(h_slabs, tile_expert, n_tiles, next_expert, sorted_row,
                       w_eg.reshape(n_layers * n_exp, d, f), w_eu.reshape(n_layers * n_exp, d, f),
                       w_ed.reshape(n_layers * n_exp, f, d), tm, max_tiles)
    return _moe_combine(y_slabs, pos[:, 0] * pitch, pos[:, 1] * pitch, x2d, info, gate, seq,
                        *next_norm, norm_dtype, emit_x)


def kernel(x, c, ada_w, ada_b, norm_mix_g, norm_ffn_g, w_in, b_forget, rel_bias, w_branch_a, w_branch_b, w_gate, b_gate, w_out, w_router_group, b_router_group, w_router_expert, b_router_expert, w_exp_gate, w_exp_up, w_exp_down, final_norm_g):
    b, s, d = x.shape
    n_layers = ada_w.shape[0]
    m = b * s
    heads_a, heads_b = b_forget.shape[1], rel_bias.shape[1]
    wa3 = 3 * heads_a * HEAD_DIM
    assert w_branch_a.shape[1] == heads_a * HEAD_DIM and w_branch_b.shape[1] == heads_b * HEAD_DIM
    assert w_in.shape[2] == wa3 + heads_a + 3 * heads_b * HEAD_DIM

    mod = _adaln_mod(c, ada_w, ada_b)
    mods = [[mod[l, :, i * d:(i + 1) * d] for i in range(6)] for l in range(n_layers)]
    w_in_t = jnp.swapaxes(w_in, 1, 2)
    zeros = jnp.zeros((b, d), F32)
    x2d = x.reshape(m, d)
    h2d = _norm_mod(x, norm_mix_g[0], mods[0][0], mods[0][1], BF16).reshape(m, d)
    for l in range(n_layers):
        _, _, gt_m, sh_f, sc_f, gt_f = mods[l]
        qkv = _in_projection(h2d, w_in_t, l, wa3, heads_a).reshape(b, s, -1)
        cum = _forget_cumsum(h2d.reshape(b, s, d), w_in_t, l, b_forget[l], wa3)
        ya = _fox_attention(qkv, cum, heads_a, (0, heads_a, 2 * heads_a))
        off = 3 * heads_a
        yb = _chunk_attention(qkv, rel_bias[l], (off, off + heads_b, off + 2 * heads_b))
        merged = _gated_merge(h2d, ya.reshape(m, -1), yb.reshape(m, -1), l, w_gate, b_gate[l],
                              w_branch_a, w_branch_b)
        x2d = _out_projection(merged, w_out, l, x2d, gt_m, s)
        last = l + 1 == n_layers
        if last:
            next_norm, norm_dtype = (final_norm_g, zeros, zeros), x.dtype
        else:
            next_norm, norm_dtype = (norm_mix_g[l + 1], mods[l + 1][0], mods[l + 1][1]), BF16
        outs = _hier_moe(x2d, s, l, (norm_ffn_g[l], sh_f, sc_f), gt_f, next_norm,
                         w_router_group[l], b_router_group[l], w_router_expert[l],
                         b_router_expert[l], w_exp_gate, w_exp_up, w_exp_down,
                         norm_dtype, emit_x=not last)
        if not last:
            x2d, h2d = outs
    return outs[0].reshape(b, s, d)
```

```python
import functools

import jax
import jax.numpy as jnp
from jax import lax
from jax.experimental import pallas as pl
from jax.experimental.pallas import tpu as pltpu

CHUNK = 64
LEFT_CHUNKS = 8
EPS = 1e-6
NEG_INF = -1e30
LANES = 128
HEAD_DIM = 128
LOG2E = 1.4426950408889634
V7X_VMEM_LIMIT_BYTES = 60000 * 1024

F32 = jnp.float32
BF16 = jnp.bfloat16


def _params(*sem):
    return pltpu.CompilerParams(dimension_semantics=sem, vmem_limit_bytes=V7X_VMEM_LIMIT_BYTES)


def _tile(n, pref, mult=LANES):
    t = (min(pref, n) // mult) * mult
    while t >= mult:
        if n % t == 0:
            return t
        t -= mult
    return n


def _cast_rows(src_ref, dst_ref, rows_per_step):
    n = src_ref.shape[0]

    def body(r, carry):
        rows = pl.ds(pl.multiple_of(r * rows_per_step, rows_per_step), rows_per_step)
        dst_ref[rows, :] = src_ref[rows, :].astype(dst_ref.dtype)
        return carry

    lax.fori_loop(0, n // rows_per_step, body, 0)


def _mod_kernel(c_ref, w_ref, b_ref, o_ref):
    c = c_ref[...]
    c_act = (c * jax.nn.sigmoid(c)).astype(BF16)
    o_ref[0] = jnp.dot(c_act, w_ref[0].astype(BF16), preferred_element_type=F32) + b_ref[0]


def _adaln_mod(c, ada_w, ada_b):
    n_layers, d, n = ada_w.shape
    b = c.shape[0]
    bp = -(-b // 8) * 8
    c_pad = jnp.pad(c, ((0, bp - b), (0, 0)))
    tn = _tile(n, 512)
    out = pl.pallas_call(
        _mod_kernel,
        grid=(n_layers, n // tn),
        in_specs=[
            pl.BlockSpec((bp, d), lambda l, j: (0, 0)),
            pl.BlockSpec((1, d, tn), lambda l, j: (l, 0, j)),
            pl.BlockSpec((1, 1, tn), lambda l, j: (l, 0, j)),
        ],
        out_specs=pl.BlockSpec((1, bp, tn), lambda l, j: (l, 0, j)),
        out_shape=jax.ShapeDtypeStruct((n_layers, bp, n), F32),
        compiler_params=_params("arbitrary", "arbitrary"),
        name="adaln_mod",
    )(c_pad, ada_w, ada_b.reshape(n_layers, 1, n))
    return out[:, :b]


def _norm_mod_kernel(x_ref, g_ref, sh_ref, sc_ref, o_ref):
    x = x_ref[0]
    ms = jnp.mean(x * x, axis=-1, keepdims=True)
    y = x * lax.rsqrt(ms + EPS) * g_ref[...]
    o_ref[0] = (y * (1.0 + sc_ref[0]) + sh_ref[0]).astype(o_ref.dtype)


def _norm_mod(x, g, shift, scale, out_dtype):
    b, s, d = x.shape
    ts = _tile(s, 256, 8)
    return pl.pallas_call(
        _norm_mod_kernel,
        grid=(b, s // ts),
        in_specs=[
            pl.BlockSpec((1, ts, d), lambda i, j: (i, j, 0)),
            pl.BlockSpec((1, d), lambda i, j: (0, 0)),
            pl.BlockSpec((1, 1, d), lambda i, j: (i, 0, 0)),
            pl.BlockSpec((1, 1, d), lambda i, j: (i, 0, 0)),
        ],
        out_specs=pl.BlockSpec((1, ts, d), lambda i, j: (i, j, 0)),
        out_shape=jax.ShapeDtypeStruct((b, s, d), out_dtype),
        compiler_params=_params("arbitrary", "arbitrary"),
        name="norm_mod",
    )(x, g.reshape(1, d), shift.reshape(b, 1, d), scale.reshape(b, 1, d))


def _dot_nt(a, b):
    return lax.dot_general(a, b, (((1,), (1,)), ((), ())), preferred_element_type=F32)


def _inproj_kernel(h_ref, w_hbm, o_ref, wbf_ref, stage_ref, sem, *, layer, n_plain, n_forget, nj, ni):
    tn = o_ref.shape[1]
    rows = tn // ni

    def src(jj, c):
        row0 = jj * tn + jnp.minimum(jj // n_plain, 1) * n_forget + c * rows
        return w_hbm.at[layer, pl.ds(pl.multiple_of(row0, 8), rows), :]

    def compute(slot):
        o_ref[...] = _dot_nt(h_ref[...], wbf_ref[slot]).astype(o_ref.dtype)

    _stream_weights(pl.program_id(0), pl.program_id(1), nj, ni,
                    [(src, stage_ref, wbf_ref, rows)], sem, compute)


def _in_projection(h2d, w_in_t, layer, w_a3, n_forget):
    m, d = h2d.shape
    n_out = w_in_t.shape[1] - n_forget
    tn = _tile(w_a3, INPROJ_TN)
    tm = _tile(m, 1024, 8)
    nj, ni = n_out // tn, m // tm
    assert n_out % tn == 0 and w_a3 % tn == 0 and n_forget % 8 == 0 and tn % (8 * ni) == 0
    kern = functools.partial(_inproj_kernel, layer=layer, n_plain=w_a3 // tn, n_forget=n_forget,
                             nj=nj, ni=ni)
    return pl.pallas_call(
        kern,
        grid=(nj, ni),
        in_specs=[
            pl.BlockSpec((tm, d), lambda j, i: (i, 0)),
            pl.BlockSpec(memory_space=pl.ANY),
        ],
        out_specs=pl.BlockSpec((tm, tn), lambda j, i: (i, j)),
        out_shape=jax.ShapeDtypeStruct((m, n_out), BF16),
        scratch_shapes=[pltpu.VMEM((2, tn, d), BF16), pltpu.VMEM((tn // ni, d), F32),
                        pltpu.SemaphoreType.DMA((1,))],
        compiler_params=_params("arbitrary", "arbitrary"),
        name="in_projection",
    )(h2d, w_in_t)


def _split3_dot(x, u):
    x1 = x.astype(BF16)
    r1 = x - x1.astype(F32)
    x2 = r1.astype(BF16)
    x3 = (r1 - x2.astype(F32)).astype(BF16)
    ub = u.astype(BF16)
    return (jnp.dot(x1, ub, preferred_element_type=F32)
            + jnp.dot(x2, ub, preferred_element_type=F32)
            + jnp.dot(x3, ub, preferred_element_type=F32))


def _forget_kernel(h_ref, w_ref, b_ref, o_ref, carry_ref):
    s = pl.program_id(1)

    @pl.when(s == 0)
    def _():
        carry_ref[...] = jnp.zeros_like(carry_ref)

    ts = h_ref.shape[1]
    f = _dot_nt(w_ref[...].astype(BF16), h_ref[0]) + b_ref[...]
    log_f = jnp.minimum(f, 0.0) - jnp.log1p(jnp.exp(-jnp.abs(f)))
    row = lax.broadcasted_iota(jnp.int32, (ts, ts), 0)
    col = lax.broadcasted_iota(jnp.int32, (ts, ts), 1)
    upper = jnp.where(row <= col, 1.0, 0.0).astype(F32)
    cs = _split3_dot(log_f, upper) + carry_ref[:, 0:1]
    o_ref[0] = cs * LOG2E
    carry_ref[...] = jnp.broadcast_to(cs[:, ts - 1:ts], carry_ref.shape)


def _forget_cumsum(h, w_in_t, layer, b_forget, row0):
    b, s, d = h.shape
    n_heads = b_forget.shape[0]
    hp = -(-n_heads // 8) * 8
    assert row0 % 8 == 0
    ts = _tile(s, 512)
    bias = jnp.pad(b_forget.astype(F32), (0, hp - n_heads)).reshape(hp, 1)
    return pl.pallas_call(
        _forget_kernel,
        grid=(b, s // ts),
        in_specs=[
            pl.BlockSpec((1, ts, d), lambda i, j: (i, j, 0)),
            pl.BlockSpec((None, pl.Element(hp), pl.Element(d)), lambda i, j: (layer, row0, 0)),
            pl.BlockSpec((hp, 1), lambda i, j: (0, 0)),
        ],
        out_specs=pl.BlockSpec((1, hp, ts), lambda i, j: (i, 0, j)),
        out_shape=jax.ShapeDtypeStruct((b, hp, s), F32),
        scratch_shapes=[pltpu.VMEM((hp, LANES), F32)],
        compiler_params=_params("arbitrary", "arbitrary"),
        name="forget_cumsum",
    )(h, w_in_t, bias)


def _fox_kernel(q_ref, k_ref, v_ref, cum_ref, o_ref, *, tq, scale_log2e):
    head = pl.program_id(1)
    seq = q_ref.shape[1]
    row = lax.broadcasted_iota(jnp.int32, (tq, tq), 0)
    col = lax.broadcasted_iota(jnp.int32, (tq, tq), 1)
    causal = row >= col
    decay = cum_ref[0, pl.ds(head, 1), :]

    n_blocks = seq // tq

    def scores(iq):
        q0 = iq * tq
        q = q_ref[0, q0:q0 + tq, :]
        s_diag = _dot_nt(q, k_ref[0, q0:q0 + tq, :]) * scale_log2e - decay[:, q0:q0 + tq]
        s_diag = jnp.where(causal, s_diag, NEG_INF)
        s_past = None
        if iq > 0:
            s_past = _dot_nt(q, k_ref[0, 0:q0, :]) * scale_log2e - decay[:, 0:q0]
        return s_diag, s_past

    nxt = scores(0)
    for iq in range(n_blocks):
        q0 = iq * tq
        s_diag, s_past = nxt
        if iq + 1 < n_blocks:
            nxt = scores(iq + 1)
        m = jnp.max(s_diag, axis=-1, keepdims=True)
        if iq > 0:
            m = jnp.maximum(m, jnp.max(s_past, axis=-1, keepdims=True))
        p_diag = jnp.exp2(s_diag - m)
        l = jnp.sum(p_diag, axis=-1, keepdims=True)
        acc = jnp.dot(p_diag.astype(BF16), v_ref[0, q0:q0 + tq, :], preferred_element_type=F32)
        if iq > 0:
            p_past = jnp.exp2(s_past - m)
            l = l + jnp.sum(p_past, axis=-1, keepdims=True)
            acc = acc + jnp.dot(p_past.astype(BF16), v_ref[0, 0:q0, :], preferred_element_type=F32)
        o_ref[0, q0:q0 + tq, :] = (acc / l).astype(o_ref.dtype)


def _fox_attention(qkv, cum, n_heads, col_blocks):
    b, s, _ = qkv.shape
    hp = cum.shape[1]
    tq = _tile(s, 256)
    qc, kc, vc = col_blocks
    kern = functools.partial(_fox_kernel, tq=tq, scale_log2e=HEAD_DIM ** -0.5 * LOG2E)
    return pl.pallas_call(
        kern,
        grid=(b, n_heads),
        in_specs=[
            pl.BlockSpec((1, s, HEAD_DIM), lambda i, h: (i, 0, qc + h)),
            pl.BlockSpec((1, s, HEAD_DIM), lambda i, h: (i, 0, kc + h)),
            pl.BlockSpec((1, s, HEAD_DIM), lambda i, h: (i, 0, vc + h)),
            pl.BlockSpec((1, hp, s), lambda i, h: (i, 0, 0)),
        ],
        out_specs=pl.BlockSpec((1, s, HEAD_DIM), lambda i, h: (i, 0, h)),
        out_shape=jax.ShapeDtypeStruct((b, s, n_heads * HEAD_DIM), BF16),
        compiler_params=_params("arbitrary", "arbitrary"),
        name="fox_attention",
    )(qkv, qkv, qkv, cum)


Q_CHUNKS = 2
QB = Q_CHUNKS * CHUNK
PAD = LEFT_CHUNKS * CHUNK
WIN = (LEFT_CHUNKS + Q_CHUNKS) * CHUNK
CHUNK_SHIFT = CHUNK.bit_length() - 1


def _chunk_bias_kernel(tab_ref, o_ref, *, max_rel, c0):
    h = pl.program_id(0)

    def visible(qi, kj):
        q_chunk = jnp.right_shift(qi, CHUNK_SHIFT)
        k_chunk = jnp.right_shift(kj, CHUNK_SHIFT)
        return jnp.logical_and(k_chunk >= q_chunk, k_chunk <= q_chunk + LEFT_CHUNKS)

    if c0 > 0:
        qi = lax.broadcasted_iota(jnp.int32, (QB, c0), 0)
        kj = lax.broadcasted_iota(jnp.int32, (QB, c0), 1)
        far = tab_ref[h, 2 * max_rel] * LOG2E
        o_ref[0, :, 0:c0] = jnp.where(visible(qi, kj), far, NEG_INF)
    qi = lax.broadcasted_iota(jnp.int32, (QB, WIN - c0), 0)
    kj = lax.broadcasted_iota(jnp.int32, (QB, WIN - c0), 1) + c0
    idx = jnp.clip(PAD + qi - kj, -max_rel, max_rel) + max_rel

    def body(r, acc):
        return jnp.where(idx == r, tab_ref[h, r], acc)

    near = lax.fori_loop(0, 2 * max_rel + 1, body, jnp.zeros((QB, WIN - c0), F32), unroll=8)
    o_ref[0, :, c0:WIN] = jnp.where(visible(qi, kj), near * LOG2E, NEG_INF)


def _chunk_bias(rel_table):
    n_heads, n_rel = rel_table.shape
    max_rel = (n_rel - 1) // 2
    c0 = max(0, (PAD - max_rel + 1) // LANES * LANES)
    return pl.pallas_call(
        functools.partial(_chunk_bias_kernel, max_rel=max_rel, c0=c0),
        grid=(n_heads,),
        in_specs=[pl.BlockSpec(memory_space=pltpu.SMEM)],
        out_specs=pl.BlockSpec((1, QB, WIN), lambda h: (h, 0, 0)),
        out_shape=jax.ShapeDtypeStruct((n_heads, QB, WIN), F32),
        compiler_params=_params("arbitrary"),
        name="chunk_bias",
    )(rel_table.astype(F32))


def _chunk_kernel(q_ref, k_ref, v_ref, bias_ref, o_ref, *, scale_log2e):
    seq = q_ref.shape[1]
    n_blocks = seq // QB

    def scores(i):
        q0 = i * QB
        k_lo = max(0, q0 - PAD)
        width = q0 + QB - k_lo
        q = q_ref[0, q0:q0 + QB, :]
        return _dot_nt(q, k_ref[0, k_lo:q0 + QB, :]) * scale_log2e + bias_ref[0, :, WIN - width:WIN]

    s_next = scores(0)
    for i in range(n_blocks):
        q0 = i * QB
        k_lo = max(0, q0 - PAD)
        s = s_next
        if i + 1 < n_blocks:
            s_next = scores(i + 1)
        m = jnp.max(s, axis=-1, keepdims=True)
        p = jnp.exp2(s - m)
        l = jnp.sum(p, axis=-1, keepdims=True)
        out = jnp.dot(p.astype(BF16), v_ref[0, k_lo:q0 + QB, :], preferred_element_type=F32)
        o_ref[0, q0:q0 + QB, :] = (out / l).astype(o_ref.dtype)


def _chunk_attention(qkv, rel_table, col_blocks):
    b, s, _ = qkv.shape
    n_heads = rel_table.shape[0]
    assert s % QB == 0
    qc, kc, vc = col_blocks
    kern = functools.partial(_chunk_kernel, scale_log2e=HEAD_DIM ** -0.5 * LOG2E)
    return pl.pallas_call(
        kern,
        grid=(b, n_heads),
        in_specs=[
            pl.BlockSpec((1, s, HEAD_DIM), lambda i, h: (i, 0, qc + h)),
            pl.BlockSpec((1, s, HEAD_DIM), lambda i, h: (i, 0, kc + h)),
            pl.BlockSpec((1, s, HEAD_DIM), lambda i, h: (i, 0, vc + h)),
            pl.BlockSpec((1, QB, WIN), lambda i, h: (h, 0, 0)),
        ],
        out_specs=pl.BlockSpec((1, s, HEAD_DIM), lambda i, h: (i, 0, h)),
        out_shape=jax.ShapeDtypeStruct((b, s, n_heads * HEAD_DIM), BF16),
        compiler_params=_params("arbitrary", "arbitrary"),
        name="chunk_attention",
    )(qkv, qkv, qkv, _chunk_bias(rel_table))


MERGE_TN = 512
INPROJ_TN = 1024


def _aligned(x, m):
    return x if isinstance(x, int) else pl.multiple_of(x, m)


def _stream_weights(j, i, nj, ni, streams, sem, compute):
    def copies(jj, c):
        return [pltpu.make_async_copy(src(jj, c), stage, sem.at[k])
                for k, (src, stage, _, _) in enumerate(streams)]

    def cast(jj, c):
        for _, stage, wbf, rows in streams:
            wbf[jj % 2, pl.ds(_aligned(c * rows, 8), rows), :] = stage[...].astype(wbf.dtype)

    @pl.when(jnp.logical_and(j == 0, i == 0))
    def _():
        for c in range(ni):
            for cp in copies(0, c):
                cp.start()
            for cp in copies(0, c):
                cp.wait()
            cast(0, c)

    @pl.when(j + 1 < nj)
    def _():
        for cp in copies(j + 1, i):
            cp.start()

    compute(j % 2)

    @pl.when(j + 1 < nj)
    def _():
        for cp in copies(j + 1, i):
            cp.wait()
        cast(j + 1, i)


def _merge_kernel(h_ref, ya_ref, yb_ref, wg_hbm, wa_hbm, wb_hbm, bga_ref, bgb_ref, o_ref,
                  wga_s, wgb_s, wa_s, wb_s, st_ga, st_gb, st_a, st_b, sem, *, layer, nb, ni):
    tn = o_ref.shape[1]

    def chunk(w_hbm, k_rows, col_block):
        rows = k_rows // ni
        return (lambda jj, c: w_hbm.at[layer, pl.ds(_aligned(c * rows, 8), rows),
                                       pl.ds(_aligned((col_block + jj) * tn, LANES), tn)]), rows

    streams = []
    for w_hbm, k_rows, col_block, stage, wbf in (
            (wg_hbm, h_ref.shape[1], 0, st_ga, wga_s), (wg_hbm, h_ref.shape[1], nb, st_gb, wgb_s),
            (wa_hbm, ya_ref.shape[1], 0, st_a, wa_s), (wb_hbm, yb_ref.shape[1], 0, st_b, wb_s)):
        src, rows = chunk(w_hbm, k_rows, col_block)
        streams.append((src, stage, wbf, rows))

    def compute(slot):
        h = h_ref[...]
        gate_a = jax.nn.sigmoid(jnp.dot(h, wga_s[slot], preferred_element_type=F32) + bga_ref[...])
        gate_b = jax.nn.sigmoid(jnp.dot(h, wgb_s[slot], preferred_element_type=F32) + bgb_ref[...])
        proj_a = jnp.dot(ya_ref[...], wa_s[slot], preferred_element_type=F32)
        proj_b = jnp.dot(yb_ref[...], wb_s[slot], preferred_element_type=F32)
        o_ref[...] = (gate_a * proj_a + gate_b * proj_b).astype(o_ref.dtype)

    _stream_weights(pl.program_id(0), pl.program_id(1), nb, ni, streams, sem, compute)


def _gated_merge(h2d, ya2d, yb2d, layer, w_gate, b_gate, w_a, w_b):
    m, d = h2d.shape
    wa_rows, wb_rows = w_a.shape[1], w_b.shape[1]
    tn = _tile(d, MERGE_TN)
    tm = _tile(m, 512, 8)
    nb, ni = d // tn, m // tm
    assert d % (8 * ni) == 0 and wa_rows % (8 * ni) == 0 and wb_rows % (8 * ni) == 0
    kern = functools.partial(_merge_kernel, layer=layer, nb=nb, ni=ni)
    b2 = b_gate.reshape(1, 2 * d)
    hbm = pl.BlockSpec(memory_space=pl.ANY)
    return pl.pallas_call(
        kern,
        grid=(nb, ni),
        in_specs=[
            pl.BlockSpec((tm, d), lambda j, i: (i, 0)),
            pl.BlockSpec((tm, wa_rows), lambda j, i: (i, 0)),
            pl.BlockSpec((tm, wb_rows), lambda j, i: (i, 0)),
            hbm, hbm, hbm,
            pl.BlockSpec((1, tn), lambda j, i: (0, j)),
            pl.BlockSpec((1, tn), lambda j, i: (0, nb + j)),
        ],
        out_specs=pl.BlockSpec((tm, tn), lambda j, i: (i, j)),
        out_shape=jax.ShapeDtypeStruct((m, d), BF16),
        scratch_shapes=[pltpu.VMEM((2, d, tn), BF16), pltpu.VMEM((2, d, tn), BF16),
                        pltpu.VMEM((2, wa_rows, tn), BF16), pltpu.VMEM((2, wb_rows, tn), BF16),
                        pltpu.VMEM((d // ni, tn), F32), pltpu.VMEM((d // ni, tn), F32),
                        pltpu.VMEM((wa_rows // ni, tn), F32), pltpu.VMEM((wb_rows // ni, tn), F32),
                        pltpu.SemaphoreType.DMA((4,))],
        compiler_params=_params("arbitrary", "arbitrary"),
        name="gated_merge",
    )(h2d, ya2d, yb2d, w_gate, w_a, w_b, b2, b2)


def _outproj_kernel(a_ref, w_ref, x_ref, gt_ref, o_ref, w_s, *, rows_per_step):
    @pl.when(pl.program_id(1) == 0)
    def _():
        _cast_rows(w_ref, w_s, rows_per_step)

    y = jnp.dot(a_ref[...], w_s[...], preferred_element_type=F32)
    o_ref[...] = x_ref[...] + gt_ref[0] * y


def _out_projection(a2d, w_out, layer, x2d, gate, seq):
    m, k = a2d.shape
    d = w_out.shape[2]
    b = gate.shape[0]
    tn = _tile(d, 512)
    tm = _tile(seq, 1024, 8)
    steps_per_batch = seq // tm
    kern = functools.partial(_outproj_kernel, rows_per_step=_tile(k, 256, 8))
    return pl.pallas_call(
        kern,
        grid=(d // tn, m // tm),
        in_specs=[
            pl.BlockSpec((tm, k), lambda j, i: (i, 0)),
            pl.BlockSpec((None, k, tn), lambda j, i: (layer, 0, j), pipeline_mode=pl.Buffered(1)),
            pl.BlockSpec((tm, tn), lambda j, i: (i, j)),
            pl.BlockSpec((1, 1, tn), lambda j, i: (i // steps_per_batch, 0, j)),
        ],
        out_specs=pl.BlockSpec((tm, tn), lambda j, i: (i, j)),
        out_shape=jax.ShapeDtypeStruct((m, d), F32),
        scratch_shapes=[pltpu.VMEM((k, tn), BF16)],
        compiler_params=_params("arbitrary", "arbitrary"),
        name="out_projection",
    )(a2d, w_out, x2d, gate.reshape(b, 1, d))


INFO_E1, INFO_E2, INFO_W1, INFO_W2, INFO_R1, INFO_R2 = range(6)


def _router_kernel(x_ref, g_ref, sh_ref, sc_ref, w_ref, b_ref, h_ref, info_ref, cnt_ref, carry_ref,
                   *, n_groups, e_per_group):
    @pl.when(pl.program_id(0) == 0)
    def _():
        carry_ref[...] = jnp.zeros_like(carry_ref)

    tm = x_ref.shape[0]
    x = x_ref[...]
    ms = jnp.mean(x * x, axis=-1, keepdims=True)
    h = x * lax.rsqrt(ms + EPS) * g_ref[...] * (1.0 + sc_ref[0]) + sh_ref[0]
    _store_slabs(h_ref, h)
    logits = jnp.dot(h.astype(BF16), w_ref[...].astype(BF16),
                     preferred_element_type=F32) + b_ref[...]
    lane = lax.broadcasted_iota(jnp.int32, (tm, LANES), 1).astype(F32)
    big = float(LANES)
    is_group = lane < n_groups
    gl = jnp.where(is_group, logits, -jnp.inf)
    g_max = jnp.max(gl, axis=-1, keepdims=True)
    g_sel = jnp.min(jnp.where(gl == g_max, lane, big), axis=-1, keepdims=True)
    g_w = 1.0 / jnp.sum(jnp.where(is_group, jnp.exp(gl - g_max), 0.0), axis=-1, keepdims=True)

    expert = lane - n_groups
    lo = g_sel * e_per_group
    in_group = jnp.logical_and(expert >= lo, expert < lo + e_per_group)
    el = jnp.where(in_group, logits, -jnp.inf)
    v1 = jnp.max(el, axis=-1, keepdims=True)
    e1 = jnp.min(jnp.where(el == v1, expert, big), axis=-1, keepdims=True)
    el2 = jnp.where(expert == e1, -jnp.inf, el)
    v2 = jnp.max(el2, axis=-1, keepdims=True)
    e2 = jnp.min(jnp.where(el2 == v2, expert, big), axis=-1, keepdims=True)
    t = jnp.exp(v2 - v1)
    w1 = g_w / (1.0 + t)
    w2 = g_w * t / (1.0 + t)

    onehot = jnp.where(jnp.logical_or(lane == e1, lane == e2), 1.0, 0.0)
    row = lax.broadcasted_iota(jnp.int32, (tm, tm), 0)
    col = lax.broadcasted_iota(jnp.int32, (tm, tm), 1)
    before = jnp.where(col < row, 1.0, 0.0).astype(BF16)
    prior = jnp.dot(before, onehot.astype(BF16), preferred_element_type=F32) + carry_ref[0:1, :]
    r1 = jnp.sum(jnp.where(lane == e1, prior, 0.0), axis=-1, keepdims=True)
    r2 = jnp.sum(jnp.where(lane == e2, prior, 0.0), axis=-1, keepdims=True)
    total = carry_ref[0:1, :] + jnp.sum(onehot, axis=0, keepdims=True)
    carry_ref[...] = jnp.broadcast_to(total, carry_ref.shape)
    cnt_ref[...] = jnp.broadcast_to(total, cnt_ref.shape)

    info = jnp.zeros((tm, LANES), F32)
    for idx, val in ((INFO_E1, e1), (INFO_E2, e2), (INFO_W1, w1), (INFO_W2, w2),
                     (INFO_R1, r1), (INFO_R2, r2)):
        info = jnp.where(lane == idx, val, info)
    info_ref[...] = info


def _norm_router(x2d, g, shift, scale, seq, w_rg, b_rg, w_re, b_re, e_per_group):
    m, d = x2d.shape
    pitch = _slab_pitch(d)
    b = shift.shape[0]
    n_groups = w_rg.shape[1]
    n_exp = w_re.shape[1]
    assert n_groups + n_exp <= LANES
    pad = LANES - n_groups - n_exp
    w = jnp.pad(jnp.concatenate([w_rg, w_re], axis=1), ((0, 0), (0, pad)))
    bias = jnp.pad(jnp.concatenate([b_rg, b_re]).astype(F32), (0, pad)).reshape(1, LANES)
    tm = _tile(seq, 256, 8)
    steps_per_batch = seq // tm
    kern = functools.partial(_router_kernel, n_groups=n_groups, e_per_group=e_per_group)
    return pl.pallas_call(
        kern,
        grid=(m // tm,),
        in_specs=[
            pl.BlockSpec((tm, d), lambda i: (i, 0)),
            pl.BlockSpec((1, d), lambda i: (0, 0)),
            pl.BlockSpec((1, 1, d), lambda i: (i // steps_per_batch, 0, 0)),
            pl.BlockSpec((1, 1, d), lambda i: (i // steps_per_batch, 0, 0)),
            pl.BlockSpec((d, LANES), lambda i: (0, 0)),
            pl.BlockSpec((1, LANES), lambda i: (0, 0)),
        ],
        out_specs=[pl.BlockSpec((tm * pitch, LANES), lambda i: (i, 0)),
                   pl.BlockSpec((tm, LANES), lambda i: (i, 0)),
                   pl.BlockSpec((8, LANES), lambda i: (0, 0))],
        out_shape=[jax.ShapeDtypeStruct((m * pitch, LANES), F32),
                   jax.ShapeDtypeStruct((m, LANES), F32), jax.ShapeDtypeStruct((8, LANES), F32)],
        scratch_shapes=[pltpu.VMEM((8, LANES), F32)],
        compiler_params=_params("arbitrary"),
        name="moe_norm_router",
    )(x2d, g.reshape(1, d), shift.reshape(b, 1, d), scale.reshape(b, 1, d), w, bias)


def _slab_rows(d):
    return d // LANES


def _slab_pitch(d):
    pitch = -(-_slab_rows(d) // 8) * 8
    return pitch if (pitch // 8) % 2 else pitch + 8


def _store_slabs(ref, val):
    n, d = val.shape
    pitch = ref.shape[0] // n
    for s in range(pitch):
        if s < _slab_rows(d):
            piece = val[:, s * LANES:(s + 1) * LANES]
        else:
            piece = jnp.zeros((n, LANES), val.dtype)
        ref[pl.ds(s, n, stride=pitch), :] = piece


def _load_slabs(ref, n, pitch, s0, s1):
    return jnp.concatenate([ref[pl.ds(s, n, stride=pitch), :] for s in range(s0, s1)], axis=1)


def _slab_copy(src_hbm, src_row, dst_ref, dst_row, rows, sem):
    return pltpu.make_async_copy(src_hbm.at[pl.ds(pl.multiple_of(src_row, 8), rows)],
                                 dst_ref.at[pl.ds(pl.multiple_of(dst_row, 8), rows)], sem)


def _gather_slabs(idx_ref, base, n, src_hbm, dst_ref, pitch, rows, sem):
    def body(r, carry):
        _slab_copy(src_hbm, idx_ref[base + r], dst_ref, r * pitch, rows, sem).start()
        return carry

    lax.fori_loop(0, n, body, 0, unroll=8)


def _wait_slabs(n, src_hbm, dst_ref, rows, sem):
    pltpu.make_async_copy(src_hbm.at[pl.ds(0, n * rows)], dst_ref.at[pl.ds(0, n * rows)], sem).wait()


def _expert_changed(te_ref, i):
    return jnp.logical_or(i == 0, te_ref[i] != te_ref[jnp.maximum(i - 1, 0)])


def _expert_weights_step(i, n_tiles, te_ref, nxt_ref, run_ref, w_hbm, wbuf, w_s, sem, rows_per_step):
    def copies(expert, slot):
        return [pltpu.make_async_copy(w.at[expert], wbuf.at[slot, k], sem.at[slot])
                for k, w in enumerate(w_hbm)]

    @pl.when(i == 0)
    def _():
        run_ref[0] = 0
        for cp in copies(te_ref[0], 0):
            cp.start(priority=1)

    @pl.when(jnp.logical_and(i < n_tiles, _expert_changed(te_ref, i)))
    def _():
        slot = run_ref[0] % 2
        for cp in copies(te_ref[i], slot):
            cp.wait()

        @pl.when(nxt_ref[i] != te_ref[i])
        def _():
            for cp in copies(nxt_ref[i], 1 - slot):
                cp.start(priority=1)

        for k, dst in enumerate(w_s):
            _cast_rows(wbuf.at[slot, k], dst, rows_per_step)
        run_ref[0] = run_ref[0] + 1


def _expert_up_kernel(te_ref, nt_ref, nxt_ref, tok_ref, h_hbm, wg_hbm, wu_hbm, a_ref, xbuf, wbuf,
                      wg_s, wu_s, sem, sem_w, run_ref, *, tm, rows_per_step, k_chunks):
    i = pl.program_id(0)
    n_tiles = nt_ref[0]
    d = wg_s.shape[0]
    rows = _slab_rows(d)
    pitch = xbuf.shape[1] // tm
    kc = d // k_chunks
    issue_chunks = max(1, k_chunks // 2)
    rows_per_chunk = tm // issue_chunks

    _expert_weights_step(i, n_tiles, te_ref, nxt_ref, run_ref, (wg_hbm, wu_hbm), wbuf,
                         (wg_s, wu_s), sem_w, rows_per_step)

    @pl.when(i == 0)
    def _():
        _gather_slabs(tok_ref, 0, tm, h_hbm, xbuf.at[0], pitch, rows, sem.at[0])

    @pl.when(i < n_tiles)
    def _():
        slot = i % 2
        nxt = 1 - slot
        _wait_slabs(tm, h_hbm, xbuf.at[slot], rows, sem.at[slot])
        g = u = None
        for c in range(k_chunks):
            if c < issue_chunks:
                for r in range(c * rows_per_chunk, (c + 1) * rows_per_chunk):
                    _slab_copy(h_hbm, tok_ref[(i + 1) * tm + r], xbuf.at[nxt], r * pitch, rows,
                               sem.at[nxt]).start()
            x = _load_slabs(xbuf.at[slot], tm, pitch, c * kc // LANES, (c + 1) * kc // LANES)
            x = x.astype(BF16)
            gc = jnp.dot(x, wg_s[c * kc:(c + 1) * kc, :], preferred_element_type=F32)
            uc = jnp.dot(x, wu_s[c * kc:(c + 1) * kc, :], preferred_element_type=F32)
            g = gc if g is None else g + gc
            u = uc if u is None else u + uc
        a_ref[...] = (g * jax.nn.sigmoid(g) * u).astype(a_ref.dtype)

    @pl.when(i == n_tiles)
    def _():
        _wait_slabs(tm, h_hbm, xbuf.at[i % 2], rows, sem.at[i % 2])

    @pl.when(i >= n_tiles)
    def _():
        a_ref[...] = jnp.zeros_like(a_ref)


def _expert_down_kernel(te_ref, nt_ref, nxt_ref, a_ref, wd_hbm, y_ref, wbuf, wd_s, sem_w, run_ref,
                        *, rows_per_step):
    i = pl.program_id(0)
    _expert_weights_step(i, nt_ref[0], te_ref, nxt_ref, run_ref, (wd_hbm,), wbuf, (wd_s,), sem_w,
                         rows_per_step)

    @pl.when(i < nt_ref[0])
    def _():
        _store_slabs(y_ref, jnp.dot(a_ref[...], wd_s[...], preferred_element_type=F32))

    @pl.when(i >= nt_ref[0])
    def _():
        y_ref[...] = jnp.zeros_like(y_ref)


def _experts(h_slabs, tile_expert, n_tiles, next_expert, sorted_row, w_gate, w_up, w_down, tm,
             max_tiles):
    n_exp, d, f = w_gate.shape
    pitch = _slab_pitch(d)
    k_chunks = min(8, d // LANES)
    assert d % (k_chunks * LANES) == 0 and tm % k_chunks == 0
    a = pl.pallas_call(
        functools.partial(_expert_up_kernel, tm=tm, rows_per_step=_tile(d, 256, 8),
                          k_chunks=k_chunks),
        grid_spec=pltpu.PrefetchScalarGridSpec(
            num_scalar_prefetch=4,
            grid=(max_tiles + 1,),
            in_specs=[
                pl.BlockSpec(memory_space=pl.ANY),
                pl.BlockSpec(memory_space=pl.ANY),
                pl.BlockSpec(memory_space=pl.ANY),
            ],
            out_specs=pl.BlockSpec((tm, f), lambda i, te, nt, nxt, tok: (i, 0)),
            scratch_shapes=[pltpu.VMEM((2, tm * pitch, LANES), F32),
                            pltpu.VMEM((2, 2, d, f), F32),
                            pltpu.VMEM((d, f), BF16), pltpu.VMEM((d, f), BF16),
                            pltpu.SemaphoreType.DMA((2,)), pltpu.SemaphoreType.DMA((2,)),
                            pltpu.SMEM((1,), jnp.int32)],
        ),
        out_shape=jax.ShapeDtypeStruct(((max_tiles + 1) * tm, f), BF16),
        compiler_params=_params("arbitrary"),
        name="moe_expert_up",
    )(tile_expert, n_tiles, next_expert, sorted_row, h_slabs, w_gate, w_up)
    return pl.pallas_call(
        functools.partial(_expert_down_kernel, rows_per_step=_tile(f, 64, 8)),
        grid_spec=pltpu.PrefetchScalarGridSpec(
            num_scalar_prefetch=3,
            grid=(max_tiles,),
            in_specs=[
                pl.BlockSpec((tm, f), lambda i, te, nt, nxt: (i, 0)),
                pl.BlockSpec(memory_space=pl.ANY),
            ],
            out_specs=pl.BlockSpec((tm * pitch, LANES), lambda i, te, nt, nxt: (i, 0)),
            scratch_shapes=[pltpu.VMEM((2, 1, f, d), F32), pltpu.VMEM((f, d), BF16),
                            pltpu.SemaphoreType.DMA((2,)), pltpu.SMEM((1,), jnp.int32)],
        ),
        out_shape=jax.ShapeDtypeStruct((max_tiles * tm * pitch, LANES), F32),
        compiler_params=_params("arbitrary"),
        name="moe_expert_down",
    )(tile_expert, n_tiles, next_expert, a, w_down)


def _combine_kernel(p1_ref, p2_ref, y_hbm, x_ref, info_ref, gt_ref, g_ref, sh_ref, sc_ref, *refs,
                    tm, emit_x, col_chunks):
    if emit_x:
        x_out, h_out, ybuf, xs, sem = refs
    else:
        h_out, ybuf, xs, sem = refs
    i = pl.program_id(0)
    n_steps = pl.num_programs(0)
    d = x_ref.shape[1]
    rows = _slab_rows(d)
    pitch = ybuf.shape[2] // tm
    cw = d // col_chunks
    issue_chunks = max(1, col_chunks // 2)
    rows_per_chunk = tm // issue_chunks

    @pl.when(i == 0)
    def _():
        _gather_slabs(p1_ref, 0, tm, y_hbm, ybuf.at[0, 0], pitch, rows, sem.at[0])
        _gather_slabs(p2_ref, 0, tm, y_hbm, ybuf.at[0, 1], pitch, rows, sem.at[0])

    slot = i % 2
    nxt = 1 - slot
    nxt_base = jnp.minimum(i + 1, n_steps - 1) * tm
    _wait_slabs(tm, y_hbm, ybuf.at[slot, 0], rows, sem.at[slot])
    _wait_slabs(tm, y_hbm, ybuf.at[slot, 1], rows, sem.at[slot])
    info = info_ref[...]
    w1 = info[:, INFO_W1:INFO_W1 + 1]
    w2 = info[:, INFO_W2:INFO_W2 + 1]
    ssq = jnp.zeros((tm, 1), F32)
    for c in range(col_chunks):
        if c < issue_chunks:
            for r in range(c * rows_per_chunk, (c + 1) * rows_per_chunk):
                for k, p_ref in enumerate((p1_ref, p2_ref)):
                    _slab_copy(y_hbm, p_ref[nxt_base + r], ybuf.at[nxt, k], r * pitch, rows,
                               sem.at[nxt]).start()
        cols = slice(c * cw, (c + 1) * cw)
        s0, s1 = c * cw // LANES, (c + 1) * cw // LANES
        y = (w1 * _load_slabs(ybuf.at[slot, 0], tm, pitch, s0, s1)
             + w2 * _load_slabs(ybuf.at[slot, 1], tm, pitch, s0, s1))
        xc = x_ref[:, cols] + gt_ref[0][:, cols] * y
        xs[:, cols] = xc
        ssq = ssq + jnp.sum(xc * xc, axis=-1, keepdims=True)
    x = xs[...]
    if emit_x:
        x_out[...] = x
    h = x * lax.rsqrt(ssq * (1.0 / d) + EPS) * g_ref[...] * (1.0 + sc_ref[0]) + sh_ref[0]
    h_out[...] = h.astype(h_out.dtype)

    @pl.when(i == n_steps - 1)
    def _():
        _wait_slabs(tm, y_hbm, ybuf.at[nxt, 0], rows, sem.at[nxt])
        _wait_slabs(tm, y_hbm, ybuf.at[nxt, 1], rows, sem.at[nxt])


def _moe_combine(y_sorted, pos1, pos2, x2d, info, gate, seq, norm_g, norm_shift, norm_scale,
                 norm_dtype, emit_x):
    m, d = x2d.shape
    b = gate.shape[0]
    tm = _tile(seq, 128, 8)
    steps_per_batch = seq // tm
    row_block = pl.BlockSpec((tm, d), lambda i, p1, p2: (i, 0))
    per_batch = pl.BlockSpec((1, 1, d), lambda i, p1, p2: (i // steps_per_batch, 0, 0))
    out_specs = [row_block, row_block] if emit_x else [row_block]
    out_shape = [jax.ShapeDtypeStruct((m, d), norm_dtype)]
    if emit_x:
        out_shape.insert(0, jax.ShapeDtypeStruct((m, d), F32))
    return pl.pallas_call(
        functools.partial(_combine_kernel, tm=tm, emit_x=emit_x, col_chunks=min(8, d // LANES)),
        grid_spec=pltpu.PrefetchScalarGridSpec(
            num_scalar_prefetch=2,
            grid=(m // tm,),
            in_specs=[
                pl.BlockSpec(memory_space=pl.ANY),
                row_block,
                pl.BlockSpec((tm, LANES), lambda i, p1, p2: (i, 0)),
                per_batch,
                pl.BlockSpec((1, d), lambda i, p1, p2: (0, 0)),
                per_batch,
                per_batch,
            ],
            out_specs=out_specs,
            scratch_shapes=[pltpu.VMEM((2, 2, tm * _slab_pitch(d), LANES), F32),
                            pltpu.VMEM((tm, d), F32), pltpu.SemaphoreType.DMA((2,))],
        ),
        out_shape=out_shape,
        compiler_params=_params("arbitrary"),
        name="moe_combine",
    )(pos1, pos2, y_sorted, x2d, info, gate.reshape(b, 1, d), norm_g.reshape(1, d),
      norm_shift.reshape(b, 1, d), norm_scale.reshape(b, 1, d))


EXPERT_TILE = 256


def _hier_moe(x2d, seq, layer, ffn_norm, gate, next_norm, w_rg, b_rg, w_re, b_re, w_eg, w_eu, w_ed,
              norm_dtype, emit_x):
    m, d = x2d.shape
    n_layers, n_groups, e_per_group, _, f = w_eg.shape
    n_exp = n_groups * e_per_group
    tm = EXPERT_TILE
    max_tiles = (2 * m) // tm + n_exp
    pitch = _slab_pitch(d)
    h_slabs, info, cnt = _norm_router(x2d, *ffn_norm, seq, w_rg, b_rg, w_re, b_re, e_per_group)

    counts = cnt[0, :n_exp].astype(jnp.int32)
    tiles_per_expert = (counts + tm - 1) // tm
    tile_end = jnp.cumsum(tiles_per_expert)
    tile_start = tile_end - tiles_per_expert
    n_tiles = tile_end[-1:]
    expert = info[:, INFO_E1:INFO_E2 + 1].astype(jnp.int32)
    rank = info[:, INFO_R1:INFO_R2 + 1].astype(jnp.int32)
    pos = tile_start[expert] * tm + rank
    tile_ids = jnp.arange(max_tiles + 1, dtype=jnp.int32)
    tile_expert = jnp.sum((tile_ids[:, None] >= tile_end[None, :]).astype(jnp.int32), axis=1)
    last_expert = tile_expert[jnp.maximum(n_tiles[0] - 1, 0)]
    tile_expert = jnp.where(tile_ids < n_tiles[0], tile_expert, last_expert)
    experts = jnp.arange(n_exp, dtype=jnp.int32)
    later = jnp.logical_and(experts[None, :] > experts[:, None], tiles_per_expert[None, :] > 0)
    following = jnp.min(jnp.where(later, experts[None, :], n_exp), axis=1)
    following = jnp.where(following == n_exp, experts, following)
    next_expert = following[tile_expert] + layer * n_exp
    tile_expert = tile_expert + layer * n_exp
    token_row = jnp.broadcast_to(jnp.arange(m, dtype=jnp.int32)[:, None] * pitch, (m, 2))
    sorted_row = jnp.zeros(((max_tiles + 1) * tm,), jnp.int32).at[pos.reshape(-1)].set(
        token_row.reshape(-1))

    y_slabs = _experts(h_slabs, tile_expert, n_tiles, next_expert, sorted_row,
                       w_eg.reshape(n_layers * n_exp, d, f), w_eu.reshape(n_layers * n_exp, d, f),
                       w_ed.reshape(n_layers * n_exp, f, d), tm, max_tiles)
    return _moe_combine(y_slabs, pos[:, 0] * pitch, pos[:, 1] * pitch, x2d, info, gate, seq,
                        *next_norm, norm_dtype, emit_x)


def kernel(x, c, ada_w, ada_b, norm_mix_g, norm_ffn_g, w_in, b_forget, rel_bias, w_branch_a, w_branch_b, w_gate, b_gate, w_out, w_router_group, b_router_group, w_router_expert, b_router_expert, w_exp_gate, w_exp_up, w_exp_down, final_norm_g):
    b, s, d = x.shape
    n_layers = ada_w.shape[0]
    m = b * s
    heads_a, heads_b = b_forget.shape[1], rel_bias.shape[1]
    wa3 = 3 * heads_a * HEAD_DIM
    assert w_branch_a.shape[1] == heads_a * HEAD_DIM and w_branch_b.shape[1] == heads_b * HEAD_DIM
    assert w_in.shape[2] == wa3 + heads_a + 3 * heads_b * HEAD_DIM

    mod = _adaln_mod(c, ada_w, ada_b)
    mods = [[mod[l, :, i * d:(i + 1) * d] for i in range(6)] for l in range(n_layers)]
    w_in_t = jnp.swapaxes(w_in, 1, 2)
    zeros = jnp.zeros((b, d), F32)
    x2d = x.reshape(m, d)
    h2d = _norm_mod(x, norm_mix_g[0], mods[0][0], mods[0][1], BF16).reshape(m, d)
    for l in range(n_layers):
        _, _, gt_m, sh_f, sc_f, gt_f = mods[l]
        qkv = _in_projection(h2d, w_in_t, l, wa3, heads_a).reshape(b, s, -1)
        cum = _forget_cumsum(h2d.reshape(b, s, d), w_in_t, l, b_forget[l], wa3)
        ya = _fox_attention(qkv, cum, heads_a, (0, heads_a, 2 * heads_a))
        off = 3 * heads_a
        yb = _chunk_attention(qkv, rel_bias[l], (off, off + heads_b, off + 2 * heads_b))
        merged = _gated_merge(h2d, ya.reshape(m, -1), yb.reshape(m, -1), l, w_gate, b_gate[l],
                              w_branch_a, w_branch_b)
        x2d = _out_projection(merged, w_out, l, x2d, gt_m, s)
        last = l + 1 == n_layers
        if last:
            next_norm, norm_dtype = (final_norm_g, zeros, zeros), x.dtype
        else:
            next_norm, norm_dtype = (norm_mix_g[l + 1], mods[l + 1][0], mods[l + 1][1]), BF16
        outs = _hier_moe(x2d, s, l, (norm_ffn_g[l], sh_f, sc_f), gt_f, next_norm,
                         w_router_group[l], b_router_group[l], w_router_expert[l],
                         b_router_expert[l], w_exp_gate, w_exp_up, w_exp_down,
                         norm_dtype, emit_x=not last)
        if not last:
            x2d, h2d = outs
    return outs[0].reshape(b, s, d)
```

```python
import functools

import jax
import jax.numpy as jnp
from jax import lax
from jax.experimental import pallas as pl
from jax.experimental.pallas import tpu as pltpu

CHUNK = 64
LEFT_CHUNKS = 8
EPS = 1e-6
NEG_INF = -1e30
LANES = 128
HEAD_DIM = 128
LOG2E = 1.4426950408889634
V7X_VMEM_LIMIT_BYTES = 60000 * 1024

F32 = jnp.float32
BF16 = jnp.bfloat16


def _params(*sem):
    return pltpu.CompilerParams(dimension_semantics=sem, vmem_limit_bytes=V7X_VMEM_LIMIT_BYTES)


def _tile(n, pref, mult=LANES):
    t = (min(pref, n) // mult) * mult
    while t >= mult:
        if n % t == 0:
            return t
        t -= mult
    return n


def _cast_rows(src_ref, dst_ref, rows_per_step):
    n = src_ref.shape[0]

    def body(r, carry):
        rows = pl.ds(pl.multiple_of(r * rows_per_step, rows_per_step), rows_per_step)
        dst_ref[rows, :] = src_ref[rows, :].astype(dst_ref.dtype)
        return carry

    lax.fori_loop(0, n // rows_per_step, body, 0)


def _mod_kernel(c_ref, w_ref, b_ref, o_ref):
    c = c_ref[...]
    c_act = (c * jax.nn.sigmoid(c)).astype(BF16)
    o_ref[0] = jnp.dot(c_act, w_ref[0].astype(BF16), preferred_element_type=F32) + b_ref[0]


def _adaln_mod(c, ada_w, ada_b):
    n_layers, d, n = ada_w.shape
    b = c.shape[0]
    bp = -(-b // 8) * 8
    c_pad = jnp.pad(c, ((0, bp - b), (0, 0)))
    tn = _tile(n, 512)
    out = pl.pallas_call(
        _mod_kernel,
        grid=(n_layers, n // tn),
        in_specs=[
            pl.BlockSpec((bp, d), lambda l, j: (0, 0)),
            pl.BlockSpec((1, d, tn), lambda l, j: (l, 0, j)),
            pl.BlockSpec((1, 1, tn), lambda l, j: (l, 0, j)),
        ],
        out_specs=pl.BlockSpec((1, bp, tn), lambda l, j: (l, 0, j)),
        out_shape=jax.ShapeDtypeStruct((n_layers, bp, n), F32),
        compiler_params=_params("arbitrary", "arbitrary"),
        name="adaln_mod",
    )(c_pad, ada_w, ada_b.reshape(n_layers, 1, n))
    return out[:, :b]


def _norm_mod_kernel(x_ref, g_ref, sh_ref, sc_ref, o_ref):
    x = x_ref[0]
    ms = jnp.mean(x * x, axis=-1, keepdims=True)
    y = x * lax.rsqrt(ms + EPS) * g_ref[...]
    o_ref[0] = (y * (1.0 + sc_ref[0]) + sh_ref[0]).astype(o_ref.dtype)


def _norm_mod(x, g, shift, scale, out_dtype):
    b, s, d = x.shape
    ts = _tile(s, 256, 8)
    return pl.pallas_call(
        _norm_mod_kernel,
        grid=(b, s // ts),
        in_specs=[
            pl.BlockSpec((1, ts, d), lambda i, j: (i, j, 0)),
            pl.BlockSpec((1, d), lambda i, j: (0, 0)),
            pl.BlockSpec((1, 1, d), lambda i, j: (i, 0, 0)),
            pl.BlockSpec((1, 1, d), lambda i, j: (i, 0, 0)),
        ],
        out_specs=pl.BlockSpec((1, ts, d), lambda i, j: (i, j, 0)),
        out_shape=jax.ShapeDtypeStruct((b, s, d), out_dtype),
        compiler_params=_params("arbitrary", "arbitrary"),
        name="norm_mod",
    )(x, g.reshape(1, d), shift.reshape(b, 1, d), scale.reshape(b, 1, d))


def _dot_nt(a, b):
    return lax.dot_general(a, b, (((1,), (1,)), ((), ())), preferred_element_type=F32)


def _inproj_kernel(h_ref, w_ref, o_ref, wbf_ref, *, rows_per_step):
    @pl.when(pl.program_id(1) == 0)
    def _():
        _cast_rows(w_ref, wbf_ref, rows_per_step)

    o_ref[...] = _dot_nt(h_ref[...], wbf_ref[...]).astype(o_ref.dtype)


def _in_projection(h2d, w_in_t, layer, w_a3, n_forget):
    m, d = h2d.shape
    n_out = w_in_t.shape[1] - n_forget
    tn = _tile(w_a3, 512)
    assert n_out % tn == 0 and w_a3 % tn == 0 and n_forget % 8 == 0
    tm = _tile(m, 1024, 8)
    n_plain = w_a3 // tn
    kern = functools.partial(_inproj_kernel, rows_per_step=_tile(tn, 128, 8))
    return pl.pallas_call(
        kern,
        grid=(n_out // tn, m // tm),
        in_specs=[
            pl.BlockSpec((tm, d), lambda j, i: (i, 0)),
            pl.BlockSpec((None, pl.Element(tn), pl.Element(d)),
                         lambda j, i: (layer, pl.multiple_of(
                             j * tn + jnp.where(j >= n_plain, n_forget, 0), 8), 0)),
        ],
        out_specs=pl.BlockSpec((tm, tn), lambda j, i: (i, j)),
        out_shape=jax.ShapeDtypeStruct((m, n_out), BF16),
        scratch_shapes=[pltpu.VMEM((tn, d), BF16)],
        compiler_params=_params("arbitrary", "arbitrary"),
        name="in_projection",
    )(h2d, w_in_t)


def _split3_dot(x, u):
    x1 = x.astype(BF16)
    r1 = x - x1.astype(F32)
    x2 = r1.astype(BF16)
    x3 = (r1 - x2.astype(F32)).astype(BF16)
    ub = u.astype(BF16)
    return (jnp.dot(x1, ub, preferred_element_type=F32)
            + jnp.dot(x2, ub, preferred_element_type=F32)
            + jnp.dot(x3, ub, preferred_element_type=F32))


def _forget_kernel(h_ref, w_ref, b_ref, o_ref, carry_ref):
    s = pl.program_id(1)

    @pl.when(s == 0)
    def _():
        carry_ref[...] = jnp.zeros_like(carry_ref)

    ts = h_ref.shape[1]
    f = _dot_nt(w_ref[...].astype(BF16), h_ref[0]) + b_ref[...]
    log_f = jnp.minimum(f, 0.0) - jnp.log1p(jnp.exp(-jnp.abs(f)))
    row = lax.broadcasted_iota(jnp.int32, (ts, ts), 0)
    col = lax.broadcasted_iota(jnp.int32, (ts, ts), 1)
    upper = jnp.where(row <= col, 1.0, 0.0).astype(F32)
    cs = _split3_dot(log_f, upper) + carry_ref[:, 0:1]
    o_ref[0] = cs * LOG2E
    carry_ref[...] = jnp.broadcast_to(cs[:, ts - 1:ts], carry_ref.shape)


def _forget_cumsum(h, w_in_t, layer, b_forget, row0):
    b, s, d = h.shape
    n_heads = b_forget.shape[0]
    hp = -(-n_heads // 8) * 8
    assert row0 % 8 == 0
    ts = _tile(s, 512)
    bias = jnp.pad(b_forget.astype(F32), (0, hp - n_heads)).reshape(hp, 1)
    return pl.pallas_call(
        _forget_kernel,
        grid=(b, s // ts),
        in_specs=[
            pl.BlockSpec((1, ts, d), lambda i, j: (i, j, 0)),
            pl.BlockSpec((None, pl.Element(hp), pl.Element(d)), lambda i, j: (layer, row0, 0)),
            pl.BlockSpec((hp, 1), lambda i, j: (0, 0)),
        ],
        out_specs=pl.BlockSpec((1, hp, ts), lambda i, j: (i, 0, j)),
        out_shape=jax.ShapeDtypeStruct((b, hp, s), F32),
        scratch_shapes=[pltpu.VMEM((hp, LANES), F32)],
        compiler_params=_params("arbitrary", "arbitrary"),
        name="forget_cumsum",
    )(h, w_in_t, bias)


def _fox_kernel(q_ref, k_ref, v_ref, cum_ref, o_ref, *, tq, scale_log2e):
    head = pl.program_id(1)
    seq = q_ref.shape[1]
    row = lax.broadcasted_iota(jnp.int32, (tq, tq), 0)
    col = lax.broadcasted_iota(jnp.int32, (tq, tq), 1)
    causal = row >= col
    decay = cum_ref[0, pl.ds(head, 1), :]

    n_blocks = seq // tq

    def scores(iq):
        q0 = iq * tq
        q = q_ref[0, q0:q0 + tq, :]
        s_diag = _dot_nt(q, k_ref[0, q0:q0 + tq, :]) * scale_log2e - decay[:, q0:q0 + tq]
        s_diag = jnp.where(causal, s_diag, NEG_INF)
        s_past = None
        if iq > 0:
            s_past = _dot_nt(q, k_ref[0, 0:q0, :]) * scale_log2e - decay[:, 0:q0]
        return s_diag, s_past

    nxt = scores(0)
    for iq in range(n_blocks):
        q0 = iq * tq
        s_diag, s_past = nxt
        if iq + 1 < n_blocks:
            nxt = scores(iq + 1)
        m = jnp.max(s_diag, axis=-1, keepdims=True)
        if iq > 0:
            m = jnp.maximum(m, jnp.max(s_past, axis=-1, keepdims=True))
        p_diag = jnp.exp2(s_diag - m)
        l = jnp.sum(p_diag, axis=-1, keepdims=True)
        acc = jnp.dot(p_diag.astype(BF16), v_ref[0, q0:q0 + tq, :], preferred_element_type=F32)
        if iq > 0:
            p_past = jnp.exp2(s_past - m)
            l = l + jnp.sum(p_past, axis=-1, keepdims=True)
            acc = acc + jnp.dot(p_past.astype(BF16), v_ref[0, 0:q0, :], preferred_element_type=F32)
        o_ref[0, q0:q0 + tq, :] = (acc / l).astype(o_ref.dtype)


def _fox_attention(qkv, cum, n_heads, col_blocks):
    b, s, _ = qkv.shape
    hp = cum.shape[1]
    tq = _tile(s, 256)
    qc, kc, vc = col_blocks
    kern = functools.partial(_fox_kernel, tq=tq, scale_log2e=HEAD_DIM ** -0.5 * LOG2E)
    return pl.pallas_call(
        kern,
        grid=(b, n_heads),
        in_specs=[
            pl.BlockSpec((1, s, HEAD_DIM), lambda i, h: (i, 0, qc + h)),
            pl.BlockSpec((1, s, HEAD_DIM), lambda i, h: (i, 0, kc + h)),
            pl.BlockSpec((1, s, HEAD_DIM), lambda i, h: (i, 0, vc + h)),
            pl.BlockSpec((1, hp, s), lambda i, h: (i, 0, 0)),
        ],
        out_specs=pl.BlockSpec((1, s, HEAD_DIM), lambda i, h: (i, 0, h)),
        out_shape=jax.ShapeDtypeStruct((b, s, n_heads * HEAD_DIM), BF16),
        compiler_params=_params("arbitrary", "arbitrary"),
        name="fox_attention",
    )(qkv, qkv, qkv, cum)


Q_CHUNKS = 2
QB = Q_CHUNKS * CHUNK
PAD = LEFT_CHUNKS * CHUNK
WIN = (LEFT_CHUNKS + Q_CHUNKS) * CHUNK
CHUNK_SHIFT = CHUNK.bit_length() - 1


def _chunk_bias_kernel(tab_ref, o_ref, *, max_rel, c0):
    h = pl.program_id(0)

    def visible(qi, kj):
        q_chunk = jnp.right_shift(qi, CHUNK_SHIFT)
        k_chunk = jnp.right_shift(kj, CHUNK_SHIFT)
        return jnp.logical_and(k_chunk >= q_chunk, k_chunk <= q_chunk + LEFT_CHUNKS)

    if c0 > 0:
        qi = lax.broadcasted_iota(jnp.int32, (QB, c0), 0)
        kj = lax.broadcasted_iota(jnp.int32, (QB, c0), 1)
        far = tab_ref[h, 2 * max_rel] * LOG2E
        o_ref[0, :, 0:c0] = jnp.where(visible(qi, kj), far, NEG_INF)
    qi = lax.broadcasted_iota(jnp.int32, (QB, WIN - c0), 0)
    kj = lax.broadcasted_iota(jnp.int32, (QB, WIN - c0), 1) + c0
    idx = jnp.clip(PAD + qi - kj, -max_rel, max_rel) + max_rel

    def body(r, acc):
        return jnp.where(idx == r, tab_ref[h, r], acc)

    near = lax.fori_loop(0, 2 * max_rel + 1, body, jnp.zeros((QB, WIN - c0), F32), unroll=8)
    o_ref[0, :, c0:WIN] = jnp.where(visible(qi, kj), near * LOG2E, NEG_INF)


def _chunk_bias(rel_table):
    n_heads, n_rel = rel_table.shape
    max_rel = (n_rel - 1) // 2
    c0 = max(0, (PAD - max_rel + 1) // LANES * LANES)
    return pl.pallas_call(
        functools.partial(_chunk_bias_kernel, max_rel=max_rel, c0=c0),
        grid=(n_heads,),
        in_specs=[pl.BlockSpec(memory_space=pltpu.SMEM)],
        out_specs=pl.BlockSpec((1, QB, WIN), lambda h: (h, 0, 0)),
        out_shape=jax.ShapeDtypeStruct((n_heads, QB, WIN), F32),
        compiler_params=_params("arbitrary"),
        name="chunk_bias",
    )(rel_table.astype(F32))


def _chunk_kernel(q_ref, k_ref, v_ref, bias_ref, o_ref, *, scale_log2e):
    seq = q_ref.shape[1]
    n_blocks = seq // QB

    def scores(i):
        q0 = i * QB
        k_lo = max(0, q0 - PAD)
        width = q0 + QB - k_lo
        q = q_ref[0, q0:q0 + QB, :]
        return _dot_nt(q, k_ref[0, k_lo:q0 + QB, :]) * scale_log2e + bias_ref[0, :, WIN - width:WIN]

    s_next = scores(0)
    for i in range(n_blocks):
        q0 = i * QB
        k_lo = max(0, q0 - PAD)
        s = s_next
        if i + 1 < n_blocks:
            s_next = scores(i + 1)
        m = jnp.max(s, axis=-1, keepdims=True)
        p = jnp.exp2(s - m)
        l = jnp.sum(p, axis=-1, keepdims=True)
        out = jnp.dot(p.astype(BF16), v_ref[0, k_lo:q0 + QB, :], preferred_element_type=F32)
        o_ref[0, q0:q0 + QB, :] = (out / l).astype(o_ref.dtype)


def _chunk_attention(qkv, rel_table, col_blocks):
    b, s, _ = qkv.shape
    n_heads = rel_table.shape[0]
    assert s % QB == 0
    qc, kc, vc = col_blocks
    kern = functools.partial(_chunk_kernel, scale_log2e=HEAD_DIM ** -0.5 * LOG2E)
    return pl.pallas_call(
        kern,
        grid=(b, n_heads),
        in_specs=[
            pl.BlockSpec((1, s, HEAD_DIM), lambda i, h: (i, 0, qc + h)),
            pl.BlockSpec((1, s, HEAD_DIM), lambda i, h: (i, 0, kc + h)),
            pl.BlockSpec((1, s, HEAD_DIM), lambda i, h: (i, 0, vc + h)),
            pl.BlockSpec((1, QB, WIN), lambda i, h: (h, 0, 0)),
        ],
        out_specs=pl.BlockSpec((1, s, HEAD_DIM), lambda i, h: (i, 0, h)),
        out_shape=jax.ShapeDtypeStruct((b, s, n_heads * HEAD_DIM), BF16),
        compiler_params=_params("arbitrary", "arbitrary"),
        name="chunk_attention",
    )(qkv, qkv, qkv, _chunk_bias(rel_table))


def _merge_kernel(h_ref, ya_ref, yb_ref, wga_ref, wgb_ref, wa_ref, wb_ref, bga_ref, bgb_ref, o_ref,
                  wga_s, wgb_s, wa_s, wb_s, *, rows_per_step):
    @pl.when(pl.program_id(1) == 0)
    def _():
        _cast_rows(wga_ref, wga_s, rows_per_step)
        _cast_rows(wgb_ref, wgb_s, rows_per_step)
        _cast_rows(wa_ref, wa_s, rows_per_step)
        _cast_rows(wb_ref, wb_s, rows_per_step)

    h = h_ref[...]
    gate_a = jax.nn.sigmoid(jnp.dot(h, wga_s[...], preferred_element_type=F32) + bga_ref[...])
    gate_b = jax.nn.sigmoid(jnp.dot(h, wgb_s[...], preferred_element_type=F32) + bgb_ref[...])
    proj_a = jnp.dot(ya_ref[...], wa_s[...], preferred_element_type=F32)
    proj_b = jnp.dot(yb_ref[...], wb_s[...], preferred_element_type=F32)
    o_ref[...] = (gate_a * proj_a + gate_b * proj_b).astype(o_ref.dtype)


def _gated_merge(h2d, ya2d, yb2d, layer, w_gate, b_gate, w_a, w_b):
    m, d = h2d.shape
    wa_rows, wb_rows = w_a.shape[1], w_b.shape[1]
    tn = _tile(d, 256)
    tm = _tile(m, 1024, 8)
    nb = d // tn
    single = pl.Buffered(1)
    kern = functools.partial(_merge_kernel, rows_per_step=_tile(min(d, wa_rows, wb_rows), 256, 8))
    b2 = b_gate.reshape(1, 2 * d)
    return pl.pallas_call(
        kern,
        grid=(nb, m // tm),
        in_specs=[
            pl.BlockSpec((tm, d), lambda j, i: (i, 0)),
            pl.BlockSpec((tm, wa_rows), lambda j, i: (i, 0)),
            pl.BlockSpec((tm, wb_rows), lambda j, i: (i, 0)),
            pl.BlockSpec((None, d, tn), lambda j, i: (layer, 0, j), pipeline_mode=single),
            pl.BlockSpec((None, d, tn), lambda j, i: (layer, 0, nb + j), pipeline_mode=single),
            pl.BlockSpec((None, wa_rows, tn), lambda j, i: (layer, 0, j), pipeline_mode=single),
            pl.BlockSpec((None, wb_rows, tn), lambda j, i: (layer, 0, j), pipeline_mode=single),
            pl.BlockSpec((1, tn), lambda j, i: (0, j)),
            pl.BlockSpec((1, tn), lambda j, i: (0, nb + j)),
        ],
        out_specs=pl.BlockSpec((tm, tn), lambda j, i: (i, j)),
        out_shape=jax.ShapeDtypeStruct((m, d), BF16),
        scratch_shapes=[pltpu.VMEM((d, tn), BF16), pltpu.VMEM((d, tn), BF16),
                        pltpu.VMEM((wa_rows, tn), BF16), pltpu.VMEM((wb_rows, tn), BF16)],
        compiler_params=_params("arbitrary", "arbitrary"),
        name="gated_merge",
    )(h2d, ya2d, yb2d, w_gate, w_gate, w_a, w_b, b2, b2)


def _outproj_kernel(a_ref, w_ref, x_ref, gt_ref, o_ref, w_s, *, rows_per_step):
    @pl.when(pl.program_id(1) == 0)
    def _():
        _cast_rows(w_ref, w_s, rows_per_step)

    y = jnp.dot(a_ref[...], w_s[...], preferred_element_type=F32)
    o_ref[...] = x_ref[...] + gt_ref[0] * y


def _out_projection(a2d, w_out, layer, x2d, gate, seq):
    m, k = a2d.shape
    d = w_out.shape[2]
    b = gate.shape[0]
    tn = _tile(d, 512)
    tm = _tile(seq, 1024, 8)
    steps_per_batch = seq // tm
    kern = functools.partial(_outproj_kernel, rows_per_step=_tile(k, 256, 8))
    return pl.pallas_call(
        kern,
        grid=(d // tn, m // tm),
        in_specs=[
            pl.BlockSpec((tm, k), lambda j, i: (i, 0)),
            pl.BlockSpec((None, k, tn), lambda j, i: (layer, 0, j), pipeline_mode=pl.Buffered(1)),
            pl.BlockSpec((tm, tn), lambda j, i: (i, j)),
            pl.BlockSpec((1, 1, tn), lambda j, i: (i // steps_per_batch, 0, j)),
        ],
        out_specs=pl.BlockSpec((tm, tn), lambda j, i: (i, j)),
        out_shape=jax.ShapeDtypeStruct((m, d), F32),
        scratch_shapes=[pltpu.VMEM((k, tn), BF16)],
        compiler_params=_params("arbitrary", "arbitrary"),
        name="out_projection",
    )(a2d, w_out, x2d, gate.reshape(b, 1, d))


INFO_E1, INFO_E2, INFO_W1, INFO_W2, INFO_R1, INFO_R2 = range(6)


def _router_kernel(x_ref, g_ref, sh_ref, sc_ref, w_ref, b_ref, h_ref, info_ref, cnt_ref, carry_ref,
                   *, n_groups, e_per_group):
    @pl.when(pl.program_id(0) == 0)
    def _():
        carry_ref[...] = jnp.zeros_like(carry_ref)

    tm = x_ref.shape[0]
    x = x_ref[...]
    ms = jnp.mean(x * x, axis=-1, keepdims=True)
    h = x * lax.rsqrt(ms + EPS) * g_ref[...] * (1.0 + sc_ref[0]) + sh_ref[0]
    _store_slabs(h_ref, h)
    logits = jnp.dot(h.astype(BF16), w_ref[...].astype(BF16),
                     preferred_element_type=F32) + b_ref[...]
    lane = lax.broadcasted_iota(jnp.int32, (tm, LANES), 1).astype(F32)
    big = float(LANES)
    is_group = lane < n_groups
    gl = jnp.where(is_group, logits, -jnp.inf)
    g_max = jnp.max(gl, axis=-1, keepdims=True)
    g_sel = jnp.min(jnp.where(gl == g_max, lane, big), axis=-1, keepdims=True)
    g_w = 1.0 / jnp.sum(jnp.where(is_group, jnp.exp(gl - g_max), 0.0), axis=-1, keepdims=True)

    expert = lane - n_groups
    lo = g_sel * e_per_group
    in_group = jnp.logical_and(expert >= lo, expert < lo + e_per_group)
    el = jnp.where(in_group, logits, -jnp.inf)
    v1 = jnp.max(el, axis=-1, keepdims=True)
    e1 = jnp.min(jnp.where(el == v1, expert, big), axis=-1, keepdims=True)
    el2 = jnp.where(expert == e1, -jnp.inf, el)
    v2 = jnp.max(el2, axis=-1, keepdims=True)
    e2 = jnp.min(jnp.where(el2 == v2, expert, big), axis=-1, keepdims=True)
    t = jnp.exp(v2 - v1)
    w1 = g_w / (1.0 + t)
    w2 = g_w * t / (1.0 + t)

    onehot = jnp.where(jnp.logical_or(lane == e1, lane == e2), 1.0, 0.0)
    row = lax.broadcasted_iota(jnp.int32, (tm, tm), 0)
    col = lax.broadcasted_iota(jnp.int32, (tm, tm), 1)
    before = jnp.where(col < row, 1.0, 0.0).astype(BF16)
    prior = jnp.dot(before, onehot.astype(BF16), preferred_element_type=F32) + carry_ref[0:1, :]
    r1 = jnp.sum(jnp.where(lane == e1, prior, 0.0), axis=-1, keepdims=True)
    r2 = jnp.sum(jnp.where(lane == e2, prior, 0.0), axis=-1, keepdims=True)
    total = carry_ref[0:1, :] + jnp.sum(onehot, axis=0, keepdims=True)
    carry_ref[...] = jnp.broadcast_to(total, carry_ref.shape)
    cnt_ref[...] = jnp.broadcast_to(total, cnt_ref.shape)

    info = jnp.zeros((tm, LANES), F32)
    for idx, val in ((INFO_E1, e1), (INFO_E2, e2), (INFO_W1, w1), (INFO_W2, w2),
                     (INFO_R1, r1), (INFO_R2, r2)):
        info = jnp.where(lane == idx, val, info)
    info_ref[...] = info


def _norm_router(x2d, g, shift, scale, seq, w_rg, b_rg, w_re, b_re, e_per_group):
    m, d = x2d.shape
    pitch = _slab_pitch(d)
    b = shift.shape[0]
    n_groups = w_rg.shape[1]
    n_exp = w_re.shape[1]
    assert n_groups + n_exp <= LANES
    pad = LANES - n_groups - n_exp
    w = jnp.pad(jnp.concatenate([w_rg, w_re], axis=1), ((0, 0), (0, pad)))
    bias = jnp.pad(jnp.concatenate([b_rg, b_re]).astype(F32), (0, pad)).reshape(1, LANES)
    tm = _tile(seq, 256, 8)
    steps_per_batch = seq // tm
    kern = functools.partial(_router_kernel, n_groups=n_groups, e_per_group=e_per_group)
    return pl.pallas_call(
        kern,
        grid=(m // tm,),
        in_specs=[
            pl.BlockSpec((tm, d), lambda i: (i, 0)),
            pl.BlockSpec((1, d), lambda i: (0, 0)),
            pl.BlockSpec((1, 1, d), lambda i: (i // steps_per_batch, 0, 0)),
            pl.BlockSpec((1, 1, d), lambda i: (i // steps_per_batch, 0, 0)),
            pl.BlockSpec((d, LANES), lambda i: (0, 0)),
            pl.BlockSpec((1, LANES), lambda i: (0, 0)),
        ],
        out_specs=[pl.BlockSpec((tm * pitch, LANES), lambda i: (i, 0)),
                   pl.BlockSpec((tm, LANES), lambda i: (i, 0)),
                   pl.BlockSpec((8, LANES), lambda i: (0, 0))],
        out_shape=[jax.ShapeDtypeStruct((m * pitch, LANES), F32),
                   jax.ShapeDtypeStruct((m, LANES), F32), jax.ShapeDtypeStruct((8, LANES), F32)],
        scratch_shapes=[pltpu.VMEM((8, LANES), F32)],
        compiler_params=_params("arbitrary"),
        name="moe_norm_router",
    )(x2d, g.reshape(1, d), shift.reshape(b, 1, d), scale.reshape(b, 1, d), w, bias)


def _slab_rows(d):
    return d // LANES


def _slab_pitch(d):
    pitch = -(-_slab_rows(d) // 8) * 8
    return pitch if (pitch // 8) % 2 else pitch + 8


def _store_slabs(ref, val):
    n, d = val.shape
    pitch = ref.shape[0] // n
    for s in range(pitch):
        if s < _slab_rows(d):
            piece = val[:, s * LANES:(s + 1) * LANES]
        else:
            piece = jnp.zeros((n, LANES), val.dtype)
        ref[pl.ds(s, n, stride=pitch), :] = piece


def _load_slabs(ref, n, pitch, s0, s1):
    return jnp.concatenate([ref[pl.ds(s, n, stride=pitch), :] for s in range(s0, s1)], axis=1)


def _slab_copy(src_hbm, src_row, dst_ref, dst_row, rows, sem):
    return pltpu.make_async_copy(src_hbm.at[pl.ds(pl.multiple_of(src_row, 8), rows)],
                                 dst_ref.at[pl.ds(pl.multiple_of(dst_row, 8), rows)], sem)


def _gather_slabs(idx_ref, base, n, src_hbm, dst_ref, pitch, rows, sem):
    def body(r, carry):
        _slab_copy(src_hbm, idx_ref[base + r], dst_ref, r * pitch, rows, sem).start()
        return carry

    lax.fori_loop(0, n, body, 0, unroll=8)


def _wait_slabs(n, src_hbm, dst_ref, rows, sem):
    pltpu.make_async_copy(src_hbm.at[pl.ds(0, n * rows)], dst_ref.at[pl.ds(0, n * rows)], sem).wait()


def _expert_changed(te_ref, i):
    return jnp.logical_or(i == 0, te_ref[i] != te_ref[jnp.maximum(i - 1, 0)])


def _expert_weights_step(i, n_tiles, te_ref, nxt_ref, run_ref, w_hbm, wbuf, w_s, sem, rows_per_step):
    def copies(expert, slot):
        return [pltpu.make_async_copy(w.at[expert], wbuf.at[slot, k], sem.at[slot])
                for k, w in enumerate(w_hbm)]

    @pl.when(i == 0)
    def _():
        run_ref[0] = 0
        for cp in copies(te_ref[0], 0):
            cp.start(priority=1)

    @pl.when(jnp.logical_and(i < n_tiles, _expert_changed(te_ref, i)))
    def _():
        slot = run_ref[0] % 2
        for cp in copies(te_ref[i], slot):
            cp.wait()

        @pl.when(nxt_ref[i] != te_ref[i])
        def _():
            for cp in copies(nxt_ref[i], 1 - slot):
                cp.start(priority=1)

        for k, dst in enumerate(w_s):
            _cast_rows(wbuf.at[slot, k], dst, rows_per_step)
        run_ref[0] = run_ref[0] + 1


def _expert_up_kernel(te_ref, nt_ref, nxt_ref, tok_ref, h_hbm, wg_hbm, wu_hbm, a_ref, xbuf, wbuf,
                      wg_s, wu_s, sem, sem_w, run_ref, *, tm, rows_per_step, k_chunks):
    i = pl.program_id(0)
    n_tiles = nt_ref[0]
    d = wg_s.shape[0]
    rows = _slab_rows(d)
    pitch = xbuf.shape[1] // tm
    kc = d // k_chunks
    issue_chunks = max(1, k_chunks // 2)
    rows_per_chunk = tm // issue_chunks

    _expert_weights_step(i, n_tiles, te_ref, nxt_ref, run_ref, (wg_hbm, wu_hbm), wbuf,
                         (wg_s, wu_s), sem_w, rows_per_step)

    @pl.when(i == 0)
    def _():
        _gather_slabs(tok_ref, 0, tm, h_hbm, xbuf.at[0], pitch, rows, sem.at[0])

    @pl.when(i < n_tiles)
    def _():
        slot = i % 2
        nxt = 1 - slot
        _wait_slabs(tm, h_hbm, xbuf.at[slot], rows, sem.at[slot])
        g = u = None
        for c in range(k_chunks):
            if c < issue_chunks:
                for r in range(c * rows_per_chunk, (c + 1) * rows_per_chunk):
                    _slab_copy(h_hbm, tok_ref[(i + 1) * tm + r], xbuf.at[nxt], r * pitch, rows,
                               sem.at[nxt]).start()
            x = _load_slabs(xbuf.at[slot], tm, pitch, c * kc // LANES, (c + 1) * kc // LANES)
            x = x.astype(BF16)
            gc = jnp.dot(x, wg_s[c * kc:(c + 1) * kc, :], preferred_element_type=F32)
            uc = jnp.dot(x, wu_s[c * kc:(c + 1) * kc, :], preferred_element_type=F32)
            g = gc if g is None else g + gc
            u = uc if u is None else u + uc
        a_ref[...] = (g * jax.nn.sigmoid(g) * u).astype(a_ref.dtype)

    @pl.when(i == n_tiles)
    def _():
        _wait_slabs(tm, h_hbm, xbuf.at[i % 2], rows, sem.at[i % 2])

    @pl.when(i >= n_tiles)
    def _():
        a_ref[...] = jnp.zeros_like(a_ref)


def _expert_down_kernel(te_ref, nt_ref, nxt_ref, a_ref, wd_hbm, y_ref, wbuf, wd_s, sem_w, run_ref,
                        *, rows_per_step):
    i = pl.program_id(0)
    _expert_weights_step(i, nt_ref[0], te_ref, nxt_ref, run_ref, (wd_hbm,), wbuf, (wd_s,), sem_w,
                         rows_per_step)

    @pl.when(i < nt_ref[0])
    def _():
        _store_slabs(y_ref, jnp.dot(a_ref[...], wd_s[...], preferred_element_type=F32))

    @pl.when(i >= nt_ref[0])
    def _():
        y_ref[...] = jnp.zeros_like(y_ref)


def _experts(h_slabs, tile_expert, n_tiles, next_expert, sorted_row, w_gate, w_up, w_down, tm,
             max_tiles):
    n_exp, d, f = w_gate.shape
    pitch = _slab_pitch(d)
    k_chunks = min(8, d // LANES)
    assert d % (k_chunks * LANES) == 0 and tm % k_chunks == 0
    a = pl.pallas_call(
        functools.partial(_expert_up_kernel, tm=tm, rows_per_step=_tile(d, 256, 8),
                          k_chunks=k_chunks),
        grid_spec=pltpu.PrefetchScalarGridSpec(
            num_scalar_prefetch=4,
            grid=(max_tiles + 1,),
            in_specs=[
                pl.BlockSpec(memory_space=pl.ANY),
                pl.BlockSpec(memory_space=pl.ANY),
                pl.BlockSpec(memory_space=pl.ANY),
            ],
            out_specs=pl.BlockSpec((tm, f), lambda i, te, nt, nxt, tok: (i, 0)),
            scratch_shapes=[pltpu.VMEM((2, tm * pitch, LANES), F32),
                            pltpu.VMEM((2, 2, d, f), F32),
                            pltpu.VMEM((d, f), BF16), pltpu.VMEM((d, f), BF16),
                            pltpu.SemaphoreType.DMA((2,)), pltpu.SemaphoreType.DMA((2,)),
                            pltpu.SMEM((1,), jnp.int32)],
        ),
        out_shape=jax.ShapeDtypeStruct(((max_tiles + 1) * tm, f), BF16),
        compiler_params=_params("arbitrary"),
        name="moe_expert_up",
    )(tile_expert, n_tiles, next_expert, sorted_row, h_slabs, w_gate, w_up)
    return pl.pallas_call(
        functools.partial(_expert_down_kernel, rows_per_step=_tile(f, 64, 8)),
        grid_spec=pltpu.PrefetchScalarGridSpec(
            num_scalar_prefetch=3,
            grid=(max_tiles,),
            in_specs=[
                pl.BlockSpec((tm, f), lambda i, te, nt, nxt: (i, 0)),
                pl.BlockSpec(memory_space=pl.ANY),
            ],
            out_specs=pl.BlockSpec((tm * pitch, LANES), lambda i, te, nt, nxt: (i, 0)),
            scratch_shapes=[pltpu.VMEM((2, 1, f, d), F32), pltpu.VMEM((f, d), BF16),
                            pltpu.SemaphoreType.DMA((2,)), pltpu.SMEM((1,), jnp.int32)],
        ),
        out_shape=jax.ShapeDtypeStruct((max_tiles * tm * pitch, LANES), F32),
        compiler_params=_params("arbitrary"),
        name="moe_expert_down",
    )(tile_expert, n_tiles, next_expert, a, w_down)


def _combine_kernel(p1_ref, p2_ref, y_hbm, x_ref, info_ref, gt_ref, g_ref, sh_ref, sc_ref, *refs,
                    tm, emit_x, col_chunks):
    if emit_x:
        x_out, h_out, ybuf, xs, sem = refs
    else:
        h_out, ybuf, xs, sem = refs
    i = pl.program_id(0)
    n_steps = pl.num_programs(0)
    d = x_ref.shape[1]
    rows = _slab_rows(d)
    pitch = ybuf.shape[2] // tm
    cw = d // col_chunks
    issue_chunks = max(1, col_chunks // 2)
    rows_per_chunk = tm // issue_chunks

    @pl.when(i == 0)
    def _():
        _gather_slabs(p1_ref, 0, tm, y_hbm, ybuf.at[0, 0], pitch, rows, sem.at[0])
        _gather_slabs(p2_ref, 0, tm, y_hbm, ybuf.at[0, 1], pitch, rows, sem.at[0])

    slot = i % 2
    nxt = 1 - slot
    nxt_base = jnp.minimum(i + 1, n_steps - 1) * tm
    _wait_slabs(tm, y_hbm, ybuf.at[slot, 0], rows, sem.at[slot])
    _wait_slabs(tm, y_hbm, ybuf.at[slot, 1], rows, sem.at[slot])
    info = info_ref[...]
    w1 = info[:, INFO_W1:INFO_W1 + 1]
    w2 = info[:, INFO_W2:INFO_W2 + 1]
    ssq = jnp.zeros((tm, 1), F32)
    for c in range(col_chunks):
        if c < issue_chunks:
            for r in range(c * rows_per_chunk, (c + 1) * rows_per_chunk):
                for k, p_ref in enumerate((p1_ref, p2_ref)):
                    _slab_copy(y_hbm, p_ref[nxt_base + r], ybuf.at[nxt, k], r * pitch, rows,
                               sem.at[nxt]).start()
        cols = slice(c * cw, (c + 1) * cw)
        s0, s1 = c * cw // LANES, (c + 1) * cw // LANES
        y = (w1 * _load_slabs(ybuf.at[slot, 0], tm, pitch, s0, s1)
             + w2 * _load_slabs(ybuf.at[slot, 1], tm, pitch, s0, s1))
        xc = x_ref[:, cols] + gt_ref[0][:, cols] * y
        xs[:, cols] = xc
        ssq = ssq + jnp.sum(xc * xc, axis=-1, keepdims=True)
    x = xs[...]
    if emit_x:
        x_out[...] = x
    h = x * lax.rsqrt(ssq * (1.0 / d) + EPS) * g_ref[...] * (1.0 + sc_ref[0]) + sh_ref[0]
    h_out[...] = h.astype(h_out.dtype)

    @pl.when(i == n_steps - 1)
    def _():
        _wait_slabs(tm, y_hbm, ybuf.at[nxt, 0], rows, sem.at[nxt])
        _wait_slabs(tm, y_hbm, ybuf.at[nxt, 1], rows, sem.at[nxt])


def _moe_combine(y_sorted, pos1, pos2, x2d, info, gate, seq, norm_g, norm_shift, norm_scale,
                 norm_dtype, emit_x):
    m, d = x2d.shape
    b = gate.shape[0]
    tm = _tile(seq, 128, 8)
    steps_per_batch = seq // tm
    row_block = pl.BlockSpec((tm, d), lambda i, p1, p2: (i, 0))
    per_batch = pl.BlockSpec((1, 1, d), lambda i, p1, p2: (i // steps_per_batch, 0, 0))
    out_specs = [row_block, row_block] if emit_x else [row_block]
    out_shape = [jax.ShapeDtypeStruct((m, d), norm_dtype)]
    if emit_x:
        out_shape.insert(0, jax.ShapeDtypeStruct((m, d), F32))
    return pl.pallas_call(
        functools.partial(_combine_kernel, tm=tm, emit_x=emit_x, col_chunks=min(8, d // LANES)),
        grid_spec=pltpu.PrefetchScalarGridSpec(
            num_scalar_prefetch=2,
            grid=(m // tm,),
            in_specs=[
                pl.BlockSpec(memory_space=pl.ANY),
                row_block,
                pl.BlockSpec((tm, LANES), lambda i, p1, p2: (i, 0)),
                per_batch,
                pl.BlockSpec((1, d), lambda i, p1, p2: (0, 0)),
                per_batch,
                per_batch,
            ],
            out_specs=out_specs,
            scratch_shapes=[pltpu.VMEM((2, 2, tm * _slab_pitch(d), LANES), F32),
                            pltpu.VMEM((tm, d), F32), pltpu.SemaphoreType.DMA((2,))],
        ),
        out_shape=out_shape,
        compiler_params=_params("arbitrary"),
        name="moe_combine",
    )(pos1, pos2, y_sorted, x2d, info, gate.reshape(b, 1, d), norm_g.reshape(1, d),
      norm_shift.reshape(b, 1, d), norm_scale.reshape(b, 1, d))


EXPERT_TILE = 256


def _hier_moe(x2d, seq, layer, ffn_norm, gate, next_norm, w_rg, b_rg, w_re, b_re, w_eg, w_eu, w_ed,
              norm_dtype, emit_x):
    m, d = x2d.shape
    n_layers, n_groups, e_per_group, _, f = w_eg.shape
    n_exp = n_groups * e_per_group
    tm = EXPERT_TILE
    max_tiles = (2 * m) // tm + n_exp
    pitch = _slab_pitch(d)
    h_slabs, info, cnt = _norm_router(x2d, *ffn_norm, seq, w_rg, b_rg, w_re, b_re, e_per_group)

    counts = cnt[0, :n_exp].astype(jnp.int32)
    tiles_per_expert = (counts + tm - 1) // tm
    tile_end = jnp.cumsum(tiles_per_expert)
    tile_start = tile_end - tiles_per_expert
    n_tiles = tile_end[-1:]
    expert = info[:, INFO_E1:INFO_E2 + 1].astype(jnp.int32)
    rank = info[:, INFO_R1:INFO_R2 + 1].astype(jnp.int32)
    pos = tile_start[expert] * tm + rank
    tile_ids = jnp.arange(max_tiles + 1, dtype=jnp.int32)
    tile_expert = jnp.sum((tile_ids[:, None] >= tile_end[None, :]).astype(jnp.int32), axis=1)
    last_expert = tile_expert[jnp.maximum(n_tiles[0] - 1, 0)]
    tile_expert = jnp.where(tile_ids < n_tiles[0], tile_expert, last_expert)
    experts = jnp.arange(n_exp, dtype=jnp.int32)
    later = jnp.logical_and(experts[None, :] > experts[:, None], tiles_per_expert[None, :] > 0)
    following = jnp.min(jnp.where(later, experts[None, :], n_exp), axis=1)
    following = jnp.where(following == n_exp, experts, following)
    next_expert = following[tile_expert] + layer * n_exp
    tile_expert = tile_expert + layer * n_exp
    token_row = jnp.broadcast_to(jnp.arange(m, dtype=jnp.int32)[:, None] * pitch, (m, 2))
    sorted_row = jnp.zeros(((max_tiles + 1) * tm,), jnp.int32).at[pos.reshape(-1)].set(
        token_row.reshape(-1))

    y_slabs = _experts(h_slabs, tile_expert, n_tiles, next_expert, sorted_row,
                       w_eg.reshape(n_layers * n_exp, d, f), w_eu.reshape(n_layers * n_exp, d, f),
                       w_ed.reshape(n_layers * n_exp, f, d), tm, max_tiles)
    return _moe_combine(y_slabs, pos[:, 0] * pitch, pos[:, 1] * pitch, x2d, info, gate, seq,
                        *next_norm, norm_dtype, emit_x)


def kernel(x, c, ada_w, ada_b, norm_mix_g, norm_ffn_g, w_in, b_forget, rel_bias, w_branch_a, w_branch_b, w_gate, b_gate, w_out, w_router_group, b_router_group, w_router_expert, b_router_expert, w_exp_gate, w_exp_up, w_exp_down, final_norm_g):
    b, s, d = x.shape
    n_layers = ada_w.shape[0]
    m = b * s
    heads_a, heads_b = b_forget.shape[1], rel_bias.shape[1]
    wa3 = 3 * heads_a * HEAD_DIM
    assert w_branch_a.shape[1] == heads_a * HEAD_DIM and w_branch_b.shape[1] == heads_b * HEAD_DIM
    assert w_in.shape[2] == wa3 + heads_a + 3 * heads_b * HEAD_DIM

    mod = _adaln_mod(c, ada_w, ada_b)
    mods = [[mod[l, :, i * d:(i + 1) * d] for i in range(6)] for l in range(n_layers)]
    w_in_t = jnp.swapaxes(w_in, 1, 2)
    zeros = jnp.zeros((b, d), F32)
    x2d = x.reshape(m, d)
    h2d = _norm_mod(x, norm_mix_g[0], mods[0][0], mods[0][1], BF16).reshape(m, d)
    for l in range(n_layers):
        _, _, gt_m, sh_f, sc_f, gt_f = mods[l]
        qkv = _in_projection(h2d, w_in_t, l, wa3, heads_a).reshape(b, s, -1)
        cum = _forget_cumsum(h2d.reshape(b, s, d), w_in_t, l, b_forget[l], wa3)
        ya = _fox_attention(qkv, cum, heads_a, (0, heads_a, 2 * heads_a))
        off = 3 * heads_a
        yb = _chunk_attention(qkv, rel_bias[l], (off, off + heads_b, off + 2 * heads_b))
        merged = _gated_merge(h2d, ya.reshape(m, -1), yb.reshape(m, -1), l, w_gate, b_gate[l],
                              w_branch_a, w_branch_b)
        x2d = _out_projection(merged, w_out, l, x2d, gt_m, s)
        last = l + 1 == n_layers
        if last:
            next_norm, norm_dtype = (final_norm_g, zeros, zeros), x.dtype
        else:
            next_norm, norm_dtype = (norm_mix_g[l + 1], mods[l + 1][0], mods[l + 1][1]), BF16
        outs = _hier_moe(x2d, s, l, (norm_ffn_g[l], sh_f, sc_f), gt_f, next_norm,
                         w_router_group[l], b_router_group[l], w_router_expert[l],
                         b_router_expert[l], w_exp_gate, w_exp_up, w_exp_down,
                         norm_dtype, emit_x=not last)
        if not last:
            x2d, h2d = outs
    return outs[0].reshape(b, s, d)
```
